```python
import jax, jax.numpy as jnp
from jax import lax
import numpy as np

D_MODEL = 1024
BATCH = 8
SEQ = 4096
DEPTH = 1

CHUNK = 64
D_MIX = D_MODEL
EPS = 1e-6
NEG_INF = -1e30

ATTN_WIDTH = D_MIX // 2
ATTN_HEAD_DIM = 64
ATTN_Q_HEADS = ATTN_WIDTH // ATTN_HEAD_DIM
ATTN_KV_HEADS = 2
ATTN_GROUP = ATTN_Q_HEADS // ATTN_KV_HEADS
WINDOW = 128
WINDOW_CHUNKS = WINDOW // CHUNK
ROPE_THETA = 10000.0

HGRN_WIDTH = D_MIX - ATTN_WIDTH
HGRN_EXPAND = 128
HGRN_HEADS = HGRN_WIDTH // HGRN_EXPAND
HGRN_KEY_DIM = HGRN_EXPAND
HGRN_VAL_DIM = HGRN_WIDTH // HGRN_HEADS

D_FF = 2816
FFN_RES_WEIGHT = 0.5

N_SUBLAYERS = 3
N_MOD = 3 * N_SUBLAYERS

PROJ_WIDTHS = (ATTN_WIDTH, ATTN_KV_HEADS * ATTN_HEAD_DIM, ATTN_KV_HEADS * ATTN_HEAD_DIM,
               HGRN_WIDTH, HGRN_WIDTH, HGRN_WIDTH, HGRN_WIDTH)
PROJ_OFFSETS = tuple(int(o) for o in np.cumsum(PROJ_WIDTHS)[:-1])
D_PROJ = int(sum(PROJ_WIDTHS))

kernel_name = "hybrid_swa_sink_hgrn2_macaron_block"


def _rmsnorm(x, gain):
    xf = x.astype(jnp.float32)
    inv = lax.rsqrt(jnp.mean(xf * xf, axis=-1, keepdims=True) + EPS)
    return (xf * inv * gain.astype(jnp.float32)).astype(x.dtype)


def _rope_tables(positions):
    inv_freq = 1.0 / (ROPE_THETA ** (jnp.arange(0, ATTN_HEAD_DIM, 2, dtype=jnp.float32) / ATTN_HEAD_DIM))
    ang = positions.astype(jnp.float32)[..., None] * inv_freq
    return jnp.cos(ang)[:, :, None, :], jnp.sin(ang)[:, :, None, :]


def _apply_rope(t, cos, sin):
    tf = t.astype(jnp.float32)
    t1, t2 = jnp.split(tf, 2, axis=-1)
    return jnp.concatenate([t1 * cos - t2 * sin, t2 * cos + t1 * sin], axis=-1).astype(t.dtype)


def _band(t):
    B, S, H, d = t.shape
    n_chunks = S // CHUNK
    tp = jnp.pad(t, ((0, 0), (WINDOW_CHUNKS * CHUNK, 0), (0, 0), (0, 0)))
    tp = tp.reshape(B, n_chunks + WINDOW_CHUNKS, CHUNK, H, d)
    return jnp.concatenate([tp[:, j:j + n_chunks] for j in range(WINDOW_CHUNKS + 1)], axis=2)


def _sliding_window_attention(q, k, v, sinks):
    B, S = q.shape[:2]
    n_chunks = S // CHUNK
    band = (WINDOW_CHUNKS + 1) * CHUNK
    qb = q.reshape(B, n_chunks, CHUNK, ATTN_KV_HEADS, ATTN_GROUP, ATTN_HEAD_DIM)
    kb = _band(k)
    vb = _band(v)
    scores = jnp.einsum("bnqhgd,bnkhd->bnhgqk", qb, kb,
                        preferred_element_type=jnp.float32) * (ATTN_HEAD_DIM ** -0.5)
    key_pos = (jnp.arange(n_chunks)[:, None] - WINDOW_CHUNKS) * CHUNK + jnp.arange(band)[None, :]
    valid = (key_pos >= 0)[None, :, None, None, None, :]
    scores = jnp.where(valid, scores, NEG_INF)
    sink_col = jnp.broadcast_to(
        sinks.astype(jnp.float32).reshape(1, 1, ATTN_KV_HEADS, ATTN_GROUP, 1, 1),
        scores.shape[:-1] + (1,))
    probs = jax.nn.softmax(jnp.concatenate([scores, sink_col], axis=-1), axis=-1)[..., :-1]
    out = jnp.einsum("bnhgqk,bnkhd->bnqhgd", probs.astype(v.dtype), vb)
    return out.reshape(B, S, ATTN_WIDTH)


def _hgrn2(hq, hf, hi, hg, lower_bound, gnorm_gain):
    B, S = hq.shape[:2]
    n_chunks = S // CHUNK
    q = jax.nn.silu(hq.astype(jnp.float32)) * (HGRN_KEY_DIM ** -0.5)
    f = lower_bound + (1.0 - lower_bound) * jax.nn.sigmoid(hf.astype(jnp.float32))
    log_f = jnp.log(f)
    k = 1.0 - f
    v = hi.astype(jnp.float32)

    def to_chunks(t, d):
        return t.reshape(B, n_chunks, CHUNK, HGRN_HEADS, d).transpose(1, 0, 3, 2, 4)

    qc = to_chunks(q, HGRN_KEY_DIM)
    kc = to_chunks(k, HGRN_KEY_DIM)
    gc = to_chunks(log_f, HGRN_KEY_DIM)
    vc = to_chunks(v, HGRN_VAL_DIM)
    causal = jnp.tril(jnp.ones((CHUNK, CHUNK), dtype=bool))[:, :, None]

    def step(state, inp):
        q_c, k_c, g_c, v_c = inp
        b = jnp.cumsum(g_c, axis=2)
        o_inter = jnp.einsum("bhtd,bhde->bhte", q_c * jnp.exp(b), state)
        diff = b[:, :, :, None, :] - b[:, :, None, :, :]
        decay = jnp.exp(jnp.where(causal, diff, -jnp.inf))
        scores = jnp.einsum("bhtd,bhsd,bhtsd->bhts", q_c, k_c, decay)
        o_intra = jnp.einsum("bhts,bhse->bhte", scores, v_c)
        b_last = b[:, :, -1:, :]
        new_state = (jnp.exp(b_last[:, :, 0, :])[..., None] * state
                     + jnp.einsum("bhsd,bhse->bhde", k_c * jnp.exp(b_last - b), v_c))
        return new_state, o_inter + o_intra

    state0 = jnp.zeros((B, HGRN_HEADS, HGRN_KEY_DIM, HGRN_VAL_DIM), jnp.float32)
    _, oc = lax.scan(step, state0, (qc, kc, gc, vc))
    o = oc.transpose(1, 0, 3, 2, 4).reshape(B, S, HGRN_HEADS, HGRN_VAL_DIM)
    o = o * lax.rsqrt(jnp.mean(o * o, axis=-1, keepdims=True) + EPS) * gnorm_gain.astype(jnp.float32)
    g = jax.nn.silu(hg.astype(jnp.float32)).reshape(B, S, HGRN_HEADS, HGRN_VAL_DIM)
    return (o * g).reshape(B, S, HGRN_WIDTH).astype(hq.dtype)


def _swiglu(h, w_in, w_out):
    gate, up = jnp.split(h @ w_in, 2, axis=-1)
    return (jax.nn.silu(gate) * up) @ w_out


def _token_mixer(h, w_in, w_out, sinks, lower_bound, gnorm_gain, cos, sin):
    B, S = h.shape[:2]
    aq, ak, av, hq, hf, hi, hg = jnp.split(h @ w_in, PROJ_OFFSETS, axis=-1)
    aq = _apply_rope(aq.reshape(B, S, ATTN_Q_HEADS, ATTN_HEAD_DIM), cos, sin)
    ak = _apply_rope(ak.reshape(B, S, ATTN_KV_HEADS, ATTN_HEAD_DIM), cos, sin)
    av = av.reshape(B, S, ATTN_KV_HEADS, ATTN_HEAD_DIM)
    attn = _sliding_window_attention(aq, ak, av, sinks)
    rec = _hgrn2(hq, hf, hi, hg, lower_bound, gnorm_gain)
    return jnp.concatenate([attn, rec], axis=-1) @ w_out


def setup_inputs(seed: int = 0) -> dict:
    key = jax.random.key(seed)
    ks = jax.random.split(key, 16)
    f32 = jnp.float32
    x = jax.random.normal(ks[0], (BATCH, SEQ, D_MODEL), f32)
    c = jax.random.normal(ks[1], (BATCH, D_MODEL), f32)
    offset = jax.random.randint(ks[2], (BATCH, 1), 0, 64, dtype=jnp.int32) * CHUNK
    positions = (offset + jnp.arange(SEQ, dtype=jnp.int32)[None, :]).astype(jnp.int32)
    w_cond = jax.random.normal(ks[3], (DEPTH, D_MODEL, N_MOD * D_MODEL), f32) * (0.5 * D_MODEL ** -0.5)
    b_cond = jax.random.normal(ks[4], (DEPTH, N_MOD * D_MODEL), f32) * 0.01
    norm_pre = 1.0 + 0.02 * jax.random.normal(ks[5], (DEPTH, N_SUBLAYERS, D_MODEL), f32)
    norm_post = 1.0 + 0.02 * jax.random.normal(ks[6], (DEPTH, N_SUBLAYERS, D_MODEL), f32)
    ffn_w_in = jax.random.normal(ks[7], (DEPTH, 2, D_MODEL, 2 * D_FF), f32) * (D_MODEL ** -0.5)
    ffn_w_out = jax.random.normal(ks[8], (DEPTH, 2, D_FF, D_MODEL), f32) * (D_FF ** -0.5)
    w_mix_in = jax.random.normal(ks[9], (DEPTH, D_MODEL, D_PROJ), f32) * (D_MODEL ** -0.5)
    w_mix_out = jax.random.normal(ks[10], (DEPTH, D_MIX, D_MODEL), f32) * (D_MIX ** -0.5)
    attn_sinks = jax.random.normal(ks[11], (DEPTH, ATTN_Q_HEADS), f32)
    hgrn_lb_logits = 1.0 + 0.1 * jax.random.normal(ks[12], (DEPTH + 1, HGRN_WIDTH), f32)
    hgrn_gnorm = 1.0 + 0.02 * jax.random.normal(ks[13], (DEPTH, HGRN_VAL_DIM), f32)
    return {"x": x, "c": c, "positions": positions, "w_cond": w_cond, "b_cond": b_cond,
            "norm_pre": norm_pre, "norm_post": norm_post, "ffn_w_in": ffn_w_in,
            "ffn_w_out": ffn_w_out, "w_mix_in": w_mix_in, "w_mix_out": w_mix_out,
            "attn_sinks": attn_sinks, "hgrn_lb_logits": hgrn_lb_logits, "hgrn_gnorm": hgrn_gnorm}


def reference(x, c, positions, w_cond, b_cond, norm_pre, norm_post, ffn_w_in, ffn_w_out,
              w_mix_in, w_mix_out, attn_sinks, hgrn_lb_logits, hgrn_gnorm):
    cos, sin = _rope_tables(positions)
    lower_bounds = jnp.cumsum(jax.nn.softmax(hgrn_lb_logits.astype(jnp.float32), axis=0), axis=0)
    c_act = jax.nn.silu(c)
    for layer in range(DEPTH):
        mod = (c_act @ w_cond[layer] + b_cond[layer])[:, None, :]
        sh1, sc1, gt1, sh2, sc2, gt2, sh3, sc3, gt3 = jnp.split(mod, N_MOD, axis=-1)
        h = _rmsnorm(x, norm_pre[layer, 0]) * (1.0 + sc1) + sh1
        y = _swiglu(h, ffn_w_in[layer, 0], ffn_w_out[layer, 0])
        x = x + FFN_RES_WEIGHT * gt1 * _rmsnorm(y, norm_post[layer, 0])
        h = _rmsnorm(x, norm_pre[layer, 1]) * (1.0 + sc2) + sh2
        y = _token_mixer(h, w_mix_in[layer], w_mix_out[layer], attn_sinks[layer],
                         lower_bounds[layer], hgrn_gnorm[layer], cos, sin)
        x = x + gt2 * _rmsnorm(y, norm_post[layer, 1])
        h = _rmsnorm(x, norm_pre[layer, 2]) * (1.0 + sc3) + sh3
        y = _swiglu(h, ffn_w_in[layer, 1], ffn_w_out[layer, 1])
        x = x + FFN_RES_WEIGHT * gt3 * _rmsnorm(y, norm_post[layer, 2])
    return x
```

```python
import functools

import jax
import jax.numpy as jnp
from jax import lax
from jax.experimental import pallas as pl
from jax.experimental.pallas import tpu as pltpu

F32 = jnp.float32
BF16 = jnp.bfloat16

EPS = 1e-6
NEG_INF = -1e30
ROPE_THETA = 10000.0

CHUNK = 64
ATTN_HEAD_DIM = 64
ATTN_Q_HEADS = 8
ATTN_KV_HEADS = 2
ATTN_GROUP = ATTN_Q_HEADS // ATTN_KV_HEADS
ATTN_WIDTH = ATTN_Q_HEADS * ATTN_HEAD_DIM
KV_WIDTH = ATTN_KV_HEADS * ATTN_HEAD_DIM
HGRN_HEADS = 4
HGRN_DIM = 128
HGRN_WIDTH = HGRN_HEADS * HGRN_DIM
N_MOD = 9
FFN_RES_WEIGHT = 0.5

OFF_AQ = 0
OFF_AKV = OFF_AQ + ATTN_WIDTH
OFF_HQ = OFF_AKV + 2 * KV_WIDTH
OFF_HF = OFF_HQ + HGRN_WIDTH
OFF_HI = OFF_HF + HGRN_WIDTH
OFF_HG = OFF_HI + HGRN_WIDTH

LANES = 128
GROUP_ROWS = 2 * CHUNK
BAND_ROWS = 4 * CHUNK
FFN_COL_TILE = 256
VMEM_LIMIT_BYTES = 56 * 1024 * 1024

DIRECT_SPAN = 4
LEVEL_HALVES = (32, 16, 8, 4)


def _sigmoid(v):
    return 1.0 / (1.0 + jnp.exp(-v))


def _rms(v):
    return v * lax.rsqrt(jnp.mean(v * v, axis=-1, keepdims=True) + EPS)


def _dot(a, b):
    return jnp.dot(a, b, preferred_element_type=F32)


def _dot_nt(a, b):
    return lax.dot_general(a, b, (((1,), (1,)), ((), ())), preferred_element_type=F32)


def _dot_tn(a, b):
    return lax.dot_general(a, b, (((0,), (0,)), ((), ())), preferred_element_type=F32)


def _mod_kernel(c_ref, w_ref, b_ref, o_ref):
    c = c_ref[...]
    ca = (c * _sigmoid(c)).astype(BF16)
    o_ref[...] = _dot(ca, w_ref[...].astype(BF16)) + b_ref[...]


def _mod_call(c, w, b):
    bsz, d = c.shape
    n = w.shape[1]
    tn = 1024
    return pl.pallas_call(
        _mod_kernel,
        grid=(n // tn,),
        in_specs=[
            pl.BlockSpec((bsz, d), lambda j: (0, 0)),
            pl.BlockSpec((d, tn), lambda j: (0, j)),
            pl.BlockSpec((1, tn), lambda j: (0, j)),
        ],
        out_specs=pl.BlockSpec((bsz, tn), lambda j: (0, j)),
        out_shape=jax.ShapeDtypeStruct((bsz, n), F32),
        compiler_params=pltpu.CompilerParams(dimension_semantics=("arbitrary",)),
        name="mod",
    )(c, w, b.reshape(1, n))


def _rope_kernel(pos_ref, inv_ref, cos_ref, sin_ref):
    ang = pos_ref[...].astype(F32) * inv_ref[...]
    cos_ref[...] = jnp.cos(ang)
    sin_ref[...] = jnp.sin(ang)


def _rope_call(pos_rep, inv_row):
    rows = pos_rep.shape[0]
    tr = min(rows, 1024)
    spec = pl.BlockSpec((tr, LANES), lambda i: (i, 0))
    return pl.pallas_call(
        _rope_kernel,
        grid=(rows // tr,),
        in_specs=[spec, pl.BlockSpec((1, LANES), lambda i: (0, 0))],
        out_specs=[spec, spec],
        out_shape=[jax.ShapeDtypeStruct((rows, LANES), F32)] * 2,
        compiler_params=pltpu.CompilerParams(dimension_semantics=("arbitrary",)),
        name="rope",
    )(pos_rep, inv_row)


def _ffn_kernel(x_ref, mod_ref, npre_ref, npost_ref, win_ref, wout_ref, o_ref, *, sub, d_ff):
    x = x_ref[0]
    sh = mod_ref[0, 3 * sub:3 * sub + 1, :]
    sc = mod_ref[0, 3 * sub + 1:3 * sub + 2, :]
    gt = mod_ref[0, 3 * sub + 2:3 * sub + 3, :]
    h = (_rms(x) * npre_ref[...] * (1.0 + sc) + sh).astype(BF16)
    acc = jnp.zeros(x.shape, F32)
    for j in range(d_ff // FFN_COL_TILE):
        lo = j * FFN_COL_TILE
        g = _dot(h, win_ref[:, lo:lo + FFN_COL_TILE])
        u = _dot(h, win_ref[:, d_ff + lo:d_ff + lo + FFN_COL_TILE])
        a = (g * _sigmoid(g) * u).astype(BF16)
        acc = acc + _dot(a, wout_ref[lo:lo + FFN_COL_TILE, :])
    o_ref[0] = x + (FFN_RES_WEIGHT * gt) * (_rms(acc) * npost_ref[...])


def _resident(shape):
    return pl.BlockSpec(shape, lambda b, s: (0,) * len(shape), pipeline_mode=pl.Buffered(1))


def _ffn_call(x, mod, npre, npost, w_in, w_out, *, sub, tm):
    bsz, seq, d = x.shape
    d_ff = w_out.shape[0]
    xspec = pl.BlockSpec((1, tm, d), lambda b, s: (b, s, 0))
    return pl.pallas_call(
        functools.partial(_ffn_kernel, sub=sub, d_ff=d_ff),
        grid=(bsz, seq // tm),
        in_specs=[
            xspec,
            pl.BlockSpec((1, N_MOD, d), lambda b, s: (b, 0, 0)),
            _resident((1, d)),
            _resident((1, d)),
            _resident((d, 2 * d_ff)),
            _resident((d_ff, d)),
        ],
        out_specs=xspec,
        out_shape=jax.ShapeDtypeStruct(x.shape, F32),
        compiler_params=pltpu.CompilerParams(
            dimension_semantics=("arbitrary", "arbitrary"),
            vmem_limit_bytes=VMEM_LIMIT_BYTES),
        name="ffn%d" % sub,
    )(x, mod, npre, npost, w_in, w_out)


def _mixer_kernel(sinks_ref, x_ref, mod_ref, npre_ref, npost_ref, win_ref, wout_ref,
                  cos_ref, sin_ref, lbl_ref, gn_ref, o_ref,
                  kbuf, vbuf, st_ref, mix_scr, b_scr, *, tm, layer):
    si = pl.program_id(1)
    n_groups = tm // GROUP_ROWS
    n_chunks = tm // CHUNK

    @pl.when(si == 0)
    def _():
        kbuf[:, 0:GROUP_ROWS, :] = jnp.zeros((ATTN_KV_HEADS, GROUP_ROWS, LANES), BF16)
        vbuf[:, 0:GROUP_ROWS, :] = jnp.zeros((ATTN_KV_HEADS, GROUP_ROWS, LANES), BF16)
        st_ref[...] = jnp.zeros(st_ref.shape, F32)

    x = x_ref[0]
    sh = mod_ref[0, 3:4, :]
    sc = mod_ref[0, 4:5, :]
    gt = mod_ref[0, 5:6, :]
    h = (_rms(x) * npre_ref[...] * (1.0 + sc) + sh).astype(BF16)

    cos = cos_ref[0]
    sin = sin_ref[0]
    lane = lax.broadcasted_iota(jnp.int32, (tm, LANES), 1)
    first_half = (lane % ATTN_HEAD_DIM) < (ATTN_HEAD_DIM // 2)
    low_head = lane < ATTN_HEAD_DIM

    def rope(t):
        rot = jnp.where(first_half, pltpu.roll(t, LANES - 32, 1), pltpu.roll(t, 32, 1))
        return t * cos + rot * sin

    aq = _dot(h, win_ref[:, OFF_AQ:OFF_AQ + ATTN_WIDTH])
    akv = _dot(h, win_ref[:, OFF_AKV:OFF_AKV + 2 * KV_WIDTH])
    k = rope(akv[:, 0:KV_WIDTH])
    v = akv[:, KV_WIDTH:2 * KV_WIDTH]
    for src, buf in ((k, kbuf), (v, vbuf)):
        buf[0, GROUP_ROWS:GROUP_ROWS + tm, :] = jnp.where(low_head, src, 0.0).astype(BF16)
        buf[1, GROUP_ROWS:GROUP_ROWS + tm, :] = jnp.where(
            low_head, pltpu.roll(src, ATTN_HEAD_DIM, 1), 0.0).astype(BF16)

    qh = []
    for jb in range(ATTN_Q_HEADS // 2):
        qb = rope(aq[:, jb * LANES:(jb + 1) * LANES]) * (ATTN_HEAD_DIM ** -0.5)
        qh.append(qb.astype(BF16))
        qh.append(pltpu.roll(qb, ATTN_HEAD_DIM, 1).astype(BF16))

    rows = ATTN_GROUP * GROUP_ROWS
    t_chunk = (lax.broadcasted_iota(jnp.int32, (rows, BAND_ROWS), 0) % GROUP_ROWS) // CHUNK
    j_col = lax.broadcasted_iota(jnp.int32, (rows, BAND_ROWS), 1)
    j_chunk = j_col // CHUNK
    band_ok = (j_chunk >= t_chunk) & (j_chunk <= t_chunk + 2)
    first_ok = band_ok & ((j_col >= GROUP_ROWS) | (si > 0))

    for g in range(n_groups):
        r0 = g * GROUP_ROWS
        valid = first_ok if g == 0 else band_ok
        for hk in range(ATTN_KV_HEADS):
            heads = [hk * ATTN_GROUP + i for i in range(ATTN_GROUP)]
            qs = jnp.concatenate([qh[hd][r0:r0 + GROUP_ROWS, :] for hd in heads], axis=0)
            sink = jnp.concatenate(
                [jnp.full((GROUP_ROWS, 1), sinks_ref[hd], F32) for hd in heads], axis=0)
            kband = kbuf[hk, r0:r0 + BAND_ROWS, :]
            vband = vbuf[hk, r0:r0 + BAND_ROWS, :]
            s = jnp.where(valid, _dot_nt(qs, kband), NEG_INF)
            m = jnp.maximum(jnp.max(s, axis=-1, keepdims=True), sink)
            p = jnp.exp(s - m)
            den = jnp.sum(p, axis=-1, keepdims=True) + jnp.exp(sink - m)
            o = _dot(p.astype(BF16), vband) / den
            for pair in range(ATTN_GROUP // 2):
                a0 = 2 * pair * GROUP_ROWS
                o_even = o[a0:a0 + GROUP_ROWS, :]
                o_odd = o[a0 + GROUP_ROWS:a0 + 2 * GROUP_ROWS, :]
                col = (hk * (ATTN_GROUP // 2) + pair) * LANES
                mix_scr[r0:r0 + GROUP_ROWS, col:col + LANES] = (
                    o_even + pltpu.roll(o_odd, ATTN_HEAD_DIM, 1)).astype(BF16)

    for hk in range(ATTN_KV_HEADS):
        kbuf[hk, 0:GROUP_ROWS, :] = kbuf[hk, tm:tm + GROUP_ROWS, :]
        vbuf[hk, 0:GROUP_ROWS, :] = vbuf[hk, tm:tm + GROUP_ROWS, :]

    lbl = lbl_ref[...]
    e_lb = jnp.exp(lbl - jnp.max(lbl, axis=0, keepdims=True))
    lb = jnp.sum(e_lb[0:layer + 1, :], axis=0, keepdims=True) / jnp.sum(e_lb, axis=0, keepdims=True)

    hq = _dot(h, win_ref[:, OFF_HQ:OFF_HQ + HGRN_WIDTH])
    hf = _dot(h, win_ref[:, OFF_HF:OFF_HF + HGRN_WIDTH])
    hi = _dot(h, win_ref[:, OFF_HI:OFF_HI + HGRN_WIDTH])
    hg = _dot(h, win_ref[:, OFF_HG:OFF_HG + HGRN_WIDTH])
    rq = hq * _sigmoid(hq) * (HGRN_DIM ** -0.5)
    f = lb + (1.0 - lb) * _sigmoid(hf)
    rk = 1.0 - f
    rv = hi.astype(BF16)
    gate = hg * _sigmoid(hg)

    rc = lax.broadcasted_iota(jnp.int32, (tm, HGRN_WIDTH), 0) % CHUNK
    b = jnp.log(f)
    shift = 1
    while shift < CHUNK:
        b = b + jnp.where(rc >= shift, pltpu.roll(b, shift, 0), 0.0)
        shift *= 2
    b_scr[...] = b

    lvl_q, lvl_k = [], []
    for half in LEVEL_HALVES:
        blk = 2 * half
        pieces = []
        for m in range(tm // blk):
            ref_row = b_scr[pl.ds(m * blk + half - 1, 1), :]
            pieces.append(jnp.broadcast_to(ref_row, (blk, HGRN_WIDTH)))
        bref = jnp.concatenate(pieces, axis=0)
        e = jnp.exp(-jnp.abs(b - bref))
        later = (rc % blk) >= half
        lvl_q.append(jnp.where(later, rq * e, 0.0).astype(BF16))
        lvl_k.append(jnp.where(later, 0.0, rk * e).astype(BF16))

    direct = []
    for delta in range(DIRECT_SPAN):
        if delta == 0:
            prod = rq * rk
        else:
            ok = (rc % DIRECT_SPAN) >= delta
            decay = jnp.exp(jnp.where(ok, b - pltpu.roll(b, delta, 0), 0.0))
            prod = jnp.where(ok, rq * pltpu.roll(rk, delta, 0) * decay, 0.0)
        direct.append([jnp.sum(prod[:, hd * HGRN_DIM:(hd + 1) * HGRN_DIM], axis=-1, keepdims=True)
                       for hd in range(HGRN_HEADS)])

    ti = lax.broadcasted_iota(jnp.int32, (CHUNK, CHUNK), 0)
    sj = lax.broadcasted_iota(jnp.int32, (CHUNK, CHUNK), 1)
    same_blk = [(ti // (2 * half)) == (sj // (2 * half)) for half in LEVEL_HALVES]
    diag = [sj == ti - delta for delta in range(DIRECT_SPAN)]

    gn = gn_ref[...]
    for hd in range(HGRN_HEADS):
        cols = slice(hd * HGRN_DIM, (hd + 1) * HGRN_DIM)
        st = st_ref[hd]
        for c in range(n_chunks):
            r0 = c * CHUNK
            scores = jnp.zeros((CHUNK, CHUNK), F32)
            for li in range(len(LEVEL_HALVES)):
                part = _dot_nt(lvl_q[li][r0:r0 + CHUNK, cols], lvl_k[li][r0:r0 + CHUNK, cols])
                scores = scores + jnp.where(same_blk[li], part, 0.0)
            for delta in range(DIRECT_SPAN):
                scores = scores + jnp.where(diag[delta], direct[delta][hd][r0:r0 + CHUNK, :], 0.0)
            b_c = b[r0:r0 + CHUNK, cols]
            b_last = b_c[CHUNK - 1:CHUNK, :]
            q_in = (rq[r0:r0 + CHUNK, cols] * jnp.exp(b_c)).astype(BF16)
            k_out = (rk[r0:r0 + CHUNK, cols] * jnp.exp(b_last - b_c)).astype(BF16)
            v_c = rv[r0:r0 + CHUNK, cols]
            o = _dot_nt(q_in, st.astype(BF16)) + _dot(scores.astype(BF16), v_c)
            st = st * jnp.exp(b_last) + _dot_tn(v_c, k_out)
            o = _rms(o) * gn * gate[r0:r0 + CHUNK, cols]
            mix_scr[r0:r0 + CHUNK, ATTN_WIDTH + hd * HGRN_DIM:ATTN_WIDTH + (hd + 1) * HGRN_DIM] = (
                o.astype(BF16))
        st_ref[hd] = st

    y = _dot(mix_scr[...], wout_ref[...])
    o_ref[0] = x + gt * (_rms(y) * npost_ref[...])


def _mixer_call(x, mod, npre, npost, w_in, w_out, cos_t, sin_t, sinks, lb_logits, gnorm, *, tm, layer):
    bsz, seq, d = x.shape
    d_proj = w_in.shape[1]
    d_mix = w_out.shape[0]
    xspec = pl.BlockSpec((1, tm, d), lambda b, s: (b, s, 0))
    tspec = pl.BlockSpec((1, tm, LANES), lambda b, s: (b, s, 0))
    return pl.pallas_call(
        functools.partial(_mixer_kernel, tm=tm, layer=layer),
        grid=(bsz, seq // tm),
        in_specs=[
            pl.BlockSpec(memory_space=pltpu.SMEM),
            xspec,
            pl.BlockSpec((1, N_MOD, d), lambda b, s: (b, 0, 0)),
            _resident((1, d)),
            _resident((1, d)),
            _resident((d, d_proj)),
            _resident((d_mix, d)),
            tspec,
            tspec,
            _resident(lb_logits.shape),
            _resident((1, HGRN_DIM)),
        ],
        out_specs=xspec,
        out_shape=jax.ShapeDtypeStruct(x.shape, F32),
        scratch_shapes=[
            pltpu.VMEM((ATTN_KV_HEADS, GROUP_ROWS + tm, LANES), BF16),
            pltpu.VMEM((ATTN_KV_HEADS, GROUP_ROWS + tm, LANES), BF16),
            pltpu.VMEM((HGRN_HEADS, HGRN_DIM, HGRN_DIM), F32),
            pltpu.VMEM((tm, d_mix), BF16),
            pltpu.VMEM((tm, HGRN_WIDTH), F32),
        ],
        compiler_params=pltpu.CompilerParams(
            dimension_semantics=("arbitrary", "arbitrary"),
            vmem_limit_bytes=VMEM_LIMIT_BYTES),
        name="mixer",
    )(sinks, x, mod, npre, npost, w_in, w_out, cos_t, sin_t, lb_logits, gnorm)


def kernel(x, c, positions, w_cond, b_cond, norm_pre, norm_post, ffn_w_in, ffn_w_out,
           w_mix_in, w_mix_out, attn_sinks, hgrn_lb_logits, hgrn_gnorm):
    bsz, seq, d = x.shape
    depth = w_cond.shape[0]
    assert seq % GROUP_ROWS == 0
    tm_ffn = min(seq, 512)
    tm_mix = min(seq, 256)

    half = ATTN_HEAD_DIM // 2
    inv_freq = 1.0 / (ROPE_THETA ** (jnp.arange(0, ATTN_HEAD_DIM, 2, dtype=F32) / ATTN_HEAD_DIM))
    inv_row = jnp.tile(inv_freq, LANES // half)[None, :]
    pos_rep = jnp.repeat(positions.reshape(-1), half).reshape(-1, LANES)
    cos_d, sin_d = _rope_call(pos_rep, inv_row)
    cos_d = cos_d.reshape(bsz, seq, half)
    sin_d = sin_d.reshape(bsz, seq, half)
    cos_t = jnp.concatenate([cos_d, cos_d, cos_d, cos_d], axis=-1)
    sin_t = jnp.concatenate([-sin_d, sin_d, -sin_d, sin_d], axis=-1)

    for layer in range(depth):
        mod = _mod_call(c, w_cond[layer], b_cond[layer]).reshape(bsz, N_MOD, d)
        npre = norm_pre[layer][:, None, :]
        npost = norm_post[layer][:, None, :]
        x = _ffn_call(x, mod, npre[0], npost[0], ffn_w_in[layer, 0].astype(BF16),
                      ffn_w_out[layer, 0].astype(BF16), sub=0, tm=tm_ffn)
        x = _mixer_call(x, mod, npre[1], npost[1], w_mix_in[layer].astype(BF16),
                        w_mix_out[layer].astype(BF16), cos_t, sin_t, attn_sinks[layer],
                        hgrn_lb_logits, hgrn_gnorm[layer][None, :], tm=tm_mix, layer=layer)
        x = _ffn_call(x, mod, npre[2], npost[2], ffn_w_in[layer, 1].astype(BF16),
                      ffn_w_out[layer, 1].astype(BF16), sub=2, tm=tm_ffn)
    return x
```

```python
import functools

import jax
import jax.numpy as jnp
from jax import lax
from jax.experimental import pallas as pl
from jax.experimental.pallas import tpu as pltpu

F32 = jnp.float32
BF16 = jnp.bfloat16

EPS = 1e-6
NEG_INF = -1e30
ROPE_THETA = 10000.0
LOG2E = 1.4426950408889634

CHUNK = 64
ATTN_HEAD_DIM = 64
ATTN_Q_HEADS = 8
ATTN_KV_HEADS = 2
ATTN_GROUP = ATTN_Q_HEADS // ATTN_KV_HEADS
ATTN_WIDTH = ATTN_Q_HEADS * ATTN_HEAD_DIM
KV_WIDTH = ATTN_KV_HEADS * ATTN_HEAD_DIM
HGRN_HEADS = 4
HGRN_DIM = 128
HGRN_WIDTH = HGRN_HEADS * HGRN_DIM
N_MOD = 9
FFN_RES_WEIGHT = 0.5

OFF_AQ = 0
OFF_AKV = OFF_AQ + ATTN_WIDTH
OFF_HQ = OFF_AKV + 2 * KV_WIDTH
OFF_HF = OFF_HQ + HGRN_WIDTH
OFF_HI = OFF_HF + HGRN_WIDTH
OFF_HG = OFF_HI + HGRN_WIDTH

LANES = 128
SUBLANES = 8
GROUP_ROWS = 2 * CHUNK
BAND_CHUNKS = 4
BAND_ROWS = BAND_CHUNKS * CHUNK
FFN_COL_TILE = 256
VMEM_LIMIT_BYTES = 56 * 1024 * 1024

FLAG_LANE = ATTN_HEAD_DIM + BAND_CHUNKS

DIRECT_SPAN = 4


def _sigmoid(v):
    return 1.0 / (1.0 + jnp.exp(-v))


def _silu(v):
    hv = 0.5 * v
    return hv + hv * jnp.tanh(hv)


def _rms(v):
    return v * lax.rsqrt(jnp.mean(v * v, axis=-1, keepdims=True) + EPS)


def _dot(a, b):
    return jnp.dot(a, b, preferred_element_type=F32)


def _dot_nt(a, b):
    return lax.dot_general(a, b, (((1,), (1,)), ((), ())), preferred_element_type=F32)


def _dot_tn(a, b):
    return lax.dot_general(a, b, (((0,), (0,)), ((), ())), preferred_element_type=F32)


def _mod_kernel(c_ref, w_ref, b_ref, o_ref):
    c = c_ref[...]
    ca = (c * _sigmoid(c)).astype(BF16)
    o_ref[...] = _dot(ca, w_ref[...].astype(BF16)) + b_ref[...]


def _mod_call(c, w, b):
    bsz, d = c.shape
    n = w.shape[1]
    tn = 1024
    return pl.pallas_call(
        _mod_kernel,
        grid=(n // tn,),
        in_specs=[
            pl.BlockSpec((bsz, d), lambda j: (0, 0)),
            pl.BlockSpec((d, tn), lambda j: (0, j)),
            pl.BlockSpec((1, tn), lambda j: (0, j)),
        ],
        out_specs=pl.BlockSpec((bsz, tn), lambda j: (0, j)),
        out_shape=jax.ShapeDtypeStruct((bsz, n), F32),
        compiler_params=pltpu.CompilerParams(dimension_semantics=("arbitrary",)),
        name="mod",
    )(c, w, b.reshape(1, n))


def _rope_kernel(pos_ref, inv_ref, cos_ref, sin_ref):
    ang = pos_ref[...].astype(F32) * inv_ref[...]
    cos_ref[...] = jnp.cos(ang)
    sin_ref[...] = jnp.sin(ang)


def _rope_call(pos_rep, inv_row):
    rows = pos_rep.shape[0]
    tr = min(rows, 1024)
    spec = pl.BlockSpec((tr, LANES), lambda i: (i, 0))
    return pl.pallas_call(
        _rope_kernel,
        grid=(rows // tr,),
        in_specs=[spec, pl.BlockSpec((1, LANES), lambda i: (0, 0))],
        out_specs=[spec, spec],
        out_shape=[jax.ShapeDtypeStruct((rows, LANES), F32)] * 2,
        compiler_params=pltpu.CompilerParams(dimension_semantics=("arbitrary",)),
        name="rope",
    )(pos_rep, inv_row)


def _ffn_kernel(x_ref, mod_ref, npre_ref, npost_ref, win_ref, wout_ref, o_ref, *, sub, d_ff):
    x = x_ref[0]
    sh = mod_ref[0, 3 * sub:3 * sub + 1, :]
    sc = mod_ref[0, 3 * sub + 1:3 * sub + 2, :]
    gt = mod_ref[0, 3 * sub + 2:3 * sub + 3, :]
    h = (_rms(x) * npre_ref[...] * (1.0 + sc) + sh).astype(BF16)
    acc = jnp.zeros(x.shape, F32)
    for j in range(d_ff // FFN_COL_TILE):
        lo = j * FFN_COL_TILE
        g = _dot(h, win_ref[:, lo:lo + FFN_COL_TILE])
        u = _dot(h, win_ref[:, d_ff + lo:d_ff + lo + FFN_COL_TILE])
        a = (g * _sigmoid(g) * u).astype(BF16)
        acc = acc + _dot(a, wout_ref[lo:lo + FFN_COL_TILE, :])
    o_ref[0] = x + (FFN_RES_WEIGHT * gt) * (_rms(acc) * npost_ref[...])


def _resident(shape):
    return pl.BlockSpec(shape, lambda b, s: (0,) * len(shape), pipeline_mode=pl.Buffered(1))


def _ffn_call(x, mod, npre, npost, w_in, w_out, *, sub, tm):
    bsz, seq, d = x.shape
    d_ff = w_out.shape[0]
    xspec = pl.BlockSpec((1, tm, d), lambda b, s: (b, s, 0))
    return pl.pallas_call(
        functools.partial(_ffn_kernel, sub=sub, d_ff=d_ff),
        grid=(bsz, seq // tm),
        in_specs=[
            xspec,
            pl.BlockSpec((1, N_MOD, d), lambda b, s: (b, 0, 0)),
            _resident((1, d)),
            _resident((1, d)),
            _resident((d, 2 * d_ff)),
            _resident((d_ff, d)),
        ],
        out_specs=xspec,
        out_shape=jax.ShapeDtypeStruct(x.shape, F32),
        compiler_params=pltpu.CompilerParams(
            dimension_semantics=("arbitrary", "arbitrary"),
            vmem_limit_bytes=VMEM_LIMIT_BYTES),
        name="ffn%d" % sub,
    )(x, mod, npre, npost, w_in, w_out)


def _mixer_kernel(sinks_ref, x_ref, mod_ref, npre_ref, npost_ref, win_ref, wout_ref,
                  cos_ref, sin_ref, lbl_ref, gn_ref, tril_ref, o_ref,
                  kbuf, vbuf, st_ref, mix_scr, b_scr, *, tm, layer):
    si = pl.program_id(1)
    n_groups = tm // GROUP_ROWS

    lane = lax.broadcasted_iota(jnp.int32, (tm, LANES), 1)
    row = lax.broadcasted_iota(jnp.int32, (tm, LANES), 0)
    low_head = lane < ATTN_HEAD_DIM

    @pl.when(si == 0)
    def _():
        carry_lane = lax.broadcasted_iota(jnp.int32, (GROUP_ROWS, LANES), 1)
        k0 = jnp.where(carry_lane == FLAG_LANE, 1.0, 0.0).astype(BF16)
        v0 = jnp.where(carry_lane < ATTN_HEAD_DIM, 0.0, 1.0).astype(BF16)
        for hk in range(ATTN_KV_HEADS):
            kbuf[hk, 0:GROUP_ROWS, :] = k0
            vbuf[hk, 0:GROUP_ROWS, :] = v0
        st_ref[...] = jnp.zeros(st_ref.shape, F32)

    x = x_ref[0]
    sh = mod_ref[0, 3:4, :]
    sc = mod_ref[0, 4:5, :]
    gt = mod_ref[0, 5:6, :]
    h = (_rms(x) * npre_ref[...] * (1.0 + sc) + sh).astype(BF16)

    cos = cos_ref[0]
    sin = sin_ref[0]
    first_half = (lane % ATTN_HEAD_DIM) < (ATTN_HEAD_DIM // 2)
    chunk_id = (row // CHUNK) % BAND_CHUNKS
    key_extra = jnp.where(lane == ATTN_HEAD_DIM + chunk_id, 1.0, 0.0)
    hidden = ATTN_HEAD_DIM + (chunk_id + 1) % BAND_CHUNKS
    query_extra = jnp.where((lane == hidden) | (lane == FLAG_LANE), NEG_INF, 0.0)

    def rope(t):
        rot = jnp.where(first_half, pltpu.roll(t, LANES - 32, 1), pltpu.roll(t, 32, 1))
        return t * cos + rot * sin

    aq = _dot(h, win_ref[:, OFF_AQ:OFF_AQ + ATTN_WIDTH])
    akv = _dot(h, win_ref[:, OFF_AKV:OFF_AKV + 2 * KV_WIDTH])
    k = rope(akv[:, 0:KV_WIDTH])
    v = akv[:, KV_WIDTH:2 * KV_WIDTH]
    for src, buf, extra in ((k, kbuf, key_extra), (v, vbuf, 1.0)):
        buf[0, GROUP_ROWS:GROUP_ROWS + tm, :] = jnp.where(low_head, src, extra).astype(BF16)
        buf[1, GROUP_ROWS:GROUP_ROWS + tm, :] = jnp.where(
            low_head, pltpu.roll(src, ATTN_HEAD_DIM, 1), extra).astype(BF16)

    qh = []
    for jb in range(ATTN_Q_HEADS // 2):
        qb = rope(aq[:, jb * LANES:(jb + 1) * LANES]) * (ATTN_HEAD_DIM ** -0.5 * LOG2E)
        qh.append(jnp.where(low_head, qb, query_extra).astype(BF16))
        qh.append(jnp.where(low_head, pltpu.roll(qb, ATTN_HEAD_DIM, 1), query_extra).astype(BF16))

    low_grp = lax.broadcasted_iota(jnp.int32, (GROUP_ROWS, LANES), 1) < ATTN_HEAD_DIM
    for g in range(n_groups):
        r0 = g * GROUP_ROWS
        for hk in range(ATTN_KV_HEADS):
            heads = [hk * ATTN_GROUP + i for i in range(ATTN_GROUP)]
            qs = jnp.concatenate([qh[hd][r0:r0 + GROUP_ROWS, :] for hd in heads], axis=0)
            kband = kbuf[hk, r0:r0 + BAND_ROWS, :]
            vband = vbuf[hk, r0:r0 + BAND_ROWS, :]
            s = _dot_nt(qs, kband)
            m = jnp.max(s, axis=-1, keepdims=True)
            p = jnp.exp2(s - m)
            o = _dot(p.astype(BF16), vband)
            for pair in range(ATTN_GROUP // 2):
                res = []
                for odd in range(2):
                    a0 = (2 * pair + odd) * GROUP_ROWS
                    o_h = o[a0:a0 + GROUP_ROWS, :]
                    sink_term = jnp.exp2(sinks_ref[heads[2 * pair + odd]] * LOG2E
                                         - m[a0:a0 + GROUP_ROWS, :])
                    swapped = pltpu.roll(o_h, ATTN_HEAD_DIM, 1)
                    if odd:
                        res.append(swapped / (o_h + sink_term))
                    else:
                        res.append(o_h / (swapped + sink_term))
                col = (hk * (ATTN_GROUP // 2) + pair) * LANES
                mix_scr[r0:r0 + GROUP_ROWS, col:col + LANES] = (
                    jnp.where(low_grp, res[0], res[1]).astype(BF16))

    for hk in range(ATTN_KV_HEADS):
        kbuf[hk, 0:GROUP_ROWS, :] = kbuf[hk, tm:tm + GROUP_ROWS, :]
        vbuf[hk, 0:GROUP_ROWS, :] = vbuf[hk, tm:tm + GROUP_ROWS, :]

    lbl = lbl_ref[...]
    e_lb = jnp.exp(lbl - jnp.max(lbl, axis=0, keepdims=True))
    lb_all = jnp.sum(e_lb[0:layer + 1, :], axis=0, keepdims=True) / jnp.sum(e_lb, axis=0, keepdims=True)

    hq = _dot(h, win_ref[:, OFF_HQ:OFF_HQ + HGRN_WIDTH])
    hf = _dot(h, win_ref[:, OFF_HF:OFF_HF + HGRN_WIDTH])
    hi = _dot(h, win_ref[:, OFF_HI:OFF_HI + HGRN_WIDTH])
    hg = _dot(h, win_ref[:, OFF_HG:OFF_HG + HGRN_WIDTH])

    f = lb_all + (1.0 - lb_all) * _sigmoid(hf)
    g = jnp.log2(f)
    g1 = g.astype(BF16)
    r1 = g - g1.astype(F32)
    g2 = r1.astype(BF16)
    g3 = (r1 - g2.astype(F32)).astype(BF16)
    tril = tril_ref[...]
    b_all = _dot(tril, g1) + _dot(tril, g2) + _dot(tril, g3)
    b_scr[...] = b_all

    half_tile = tm // 2
    level_halves = []
    span = half_tile
    while span >= DIRECT_SPAN:
        level_halves.append(span)
        span //= 2
    ti = lax.broadcasted_iota(jnp.int32, (half_tile, half_tile), 0)
    sj = lax.broadcasted_iota(jnp.int32, (half_tile, half_tile), 1)
    quad_mask = {}
    for half in level_halves[1:]:
        blk = 2 * half
        quad_mask[half] = ((ti // blk) == (sj // blk)) & ((ti % blk) >= half) & ((sj % blk) < half)
    diag = [sj == ti - delta for delta in range(DIRECT_SPAN)]
    sub_row = row % SUBLANES
    direct_ok = [(sub_row % DIRECT_SPAN) >= delta for delta in range(DIRECT_SPAN)]
    later4 = (sub_row % (2 * DIRECT_SPAN)) >= DIRECT_SPAN

    gn = gn_ref[...]
    for hd in range(HGRN_HEADS):
        cols = slice(hd * HGRN_DIM, (hd + 1) * HGRN_DIM)
        rq = _silu(hq[:, cols]) * (HGRN_DIM ** -0.5)
        rk = 1.0 - f[:, cols]
        rv = hi[:, cols].astype(BF16)
        gate = _silu(hg[:, cols])
        b = b_all[:, cols]

        lvl_w = {}
        for half in level_halves:
            blk = 2 * half
            expo, operand = [], []
            for mblk in range(tm // blk):
                lo = mblk * blk
                ref_row = b_scr[pl.ds(lo + half - 1, 1), cols]
                if half >= SUBLANES:
                    ref_half = jnp.broadcast_to(ref_row, (half, HGRN_DIM))
                    expo += [ref_half - b[lo:lo + half, :], b[lo + half:lo + blk, :] - ref_half]
                    operand += [rk[lo:lo + half, :], rq[lo + half:lo + blk, :]]
                else:
                    expo.append(jnp.broadcast_to(ref_row, (blk, HGRN_DIM)))
            if half >= SUBLANES:
                w = jnp.concatenate(operand, axis=0) * jnp.exp2(jnp.concatenate(expo, axis=0))
            else:
                dist = jnp.abs(b - jnp.concatenate(expo, axis=0))
                w = jnp.where(later4, rq, rk) * jnp.exp2(-dist)
            lvl_w[half] = w.astype(BF16)

        def shifted(t, delta):
            t3 = t.reshape(tm // SUBLANES, SUBLANES, HGRN_DIM)
            return pltpu.roll(t3, delta, 1).reshape(tm, HGRN_DIM)

        direct = []
        for delta in range(DIRECT_SPAN):
            if delta == 0:
                prod = rq * rk
            else:
                ok = direct_ok[delta]
                decay = jnp.exp2(jnp.where(ok, b - shifted(b, delta), 0.0))
                prod = jnp.where(ok, rq * shifted(rk, delta) * decay, 0.0)
            direct.append(jnp.sum(prod, axis=-1, keepdims=True))

        top = _dot_nt(lvl_w[half_tile][half_tile:, :], lvl_w[half_tile][0:half_tile, :])
        lvl_p = {half: _dot_nt(lvl_w[half], lvl_w[half]) for half in level_halves[1:]}
        quads = []
        for qd in range(2):
            o0 = qd * half_tile
            sc_q = jnp.zeros((half_tile, half_tile), F32)
            for delta in range(DIRECT_SPAN):
                sc_q = jnp.where(diag[delta], direct[delta][o0:o0 + half_tile, :], sc_q)
            for half in level_halves[1:]:
                sc_q = jnp.where(quad_mask[half], lvl_p[half][o0:o0 + half_tile, o0:o0 + half_tile], sc_q)
            quads.append(sc_q.astype(BF16))
        s_hi = jnp.concatenate([top.astype(BF16), quads[1]], axis=1)

        st = st_ref[hd]
        b_last = b[tm - 1:tm, :]
        q_in = (rq * jnp.exp2(b)).astype(BF16)
        k_out = (rk * jnp.exp2(b_last - b)).astype(BF16)
        o = _dot_nt(q_in, st.astype(BF16)) + jnp.concatenate(
            [_dot(quads[0], rv[0:half_tile, :]), _dot(s_hi, rv)], axis=0)
        st_ref[hd] = st * jnp.exp2(b_last) + _dot_tn(rv, k_out)
        o = _rms(o) * gn * gate
        mix_scr[:, ATTN_WIDTH + hd * HGRN_DIM:ATTN_WIDTH + (hd + 1) * HGRN_DIM] = o.astype(BF16)

    y = _dot(mix_scr[...], wout_ref[...])
    o_ref[0] = x + gt * (_rms(y) * npost_ref[...])


def _mixer_call(x, mod, npre, npost, w_in, w_out, cos_t, sin_t, sinks, lb_logits, gnorm, *, tm, layer):
    bsz, seq, d = x.shape
    d_proj = w_in.shape[1]
    d_mix = w_out.shape[0]
    assert tm % (BAND_CHUNKS * CHUNK) == 0
    r = jnp.arange(tm)
    tril = (r[None, :] <= r[:, None]).astype(BF16)
    xspec = pl.BlockSpec((1, tm, d), lambda b, s: (b, s, 0))
    tspec = pl.BlockSpec((1, tm, LANES), lambda b, s: (b, s, 0))
    return pl.pallas_call(
        functools.partial(_mixer_kernel, tm=tm, layer=layer),
        grid=(bsz, seq // tm),
        in_specs=[
            pl.BlockSpec(memory_space=pltpu.SMEM),
            xspec,
            pl.BlockSpec((1, N_MOD, d), lambda b, s: (b, 0, 0)),
            _resident((1, d)),
            _resident((1, d)),
            _resident((d, d_proj)),
            _resident((d_mix, d)),
            tspec,
            tspec,
            _resident(lb_logits.shape),
            _resident((1, HGRN_DIM)),
            _resident((tm, tm)),
        ],
        out_specs=xspec,
        out_shape=jax.ShapeDtypeStruct(x.shape, F32),
        scratch_shapes=[
            pltpu.VMEM((ATTN_KV_HEADS, GROUP_ROWS + tm, LANES), BF16),
            pltpu.VMEM((ATTN_KV_HEADS, GROUP_ROWS + tm, LANES), BF16),
            pltpu.VMEM((HGRN_HEADS, HGRN_DIM, HGRN_DIM), F32),
            pltpu.VMEM((tm, d_mix), BF16),
            pltpu.VMEM((tm, HGRN_WIDTH), F32),
        ],
        compiler_params=pltpu.CompilerParams(
            dimension_semantics=("arbitrary", "arbitrary"),
            vmem_limit_bytes=VMEM_LIMIT_BYTES),
        name="mixer",
    )(sinks, x, mod, npre, npost, w_in, w_out, cos_t, sin_t, lb_logits, gnorm, tril)


def kernel(x, c, positions, w_cond, b_cond, norm_pre, norm_post, ffn_w_in, ffn_w_out,
           w_mix_in, w_mix_out, attn_sinks, hgrn_lb_logits, hgrn_gnorm):
    bsz, seq, d = x.shape
    depth = w_cond.shape[0]
    tm_ffn = min(seq, 512)
    tm_mix = min(seq, 256)

    half = ATTN_HEAD_DIM // 2
    inv_freq = 1.0 / (ROPE_THETA ** (jnp.arange(0, ATTN_HEAD_DIM, 2, dtype=F32) / ATTN_HEAD_DIM))
    inv_row = jnp.tile(inv_freq, LANES // half)[None, :]
    pos_rep = jnp.broadcast_to(positions.reshape(-1, 1), (bsz * seq, half)).reshape(-1, LANES)
    cos_d, sin_d = _rope_call(pos_rep, inv_row)
    cos_d = cos_d.reshape(bsz, seq, half)
    sin_d = sin_d.reshape(bsz, seq, half)
    sign = jnp.concatenate([-jnp.ones((half,), F32), jnp.ones((half,), F32)] * 2)
    cos_t = jnp.concatenate([cos_d] * 4, axis=-1)
    sin_t = jnp.concatenate([sin_d] * 4, axis=-1) * sign

    for layer in range(depth):
        mod = _mod_call(c, w_cond[layer], b_cond[layer]).reshape(bsz, N_MOD, d)
        npre = norm_pre[layer][:, None, :]
        npost = norm_post[layer][:, None, :]
        x = _ffn_call(x, mod, npre[0], npost[0], ffn_w_in[layer, 0].astype(BF16),
                      ffn_w_out[layer, 0].astype(BF16), sub=0, tm=tm_ffn)
        x = _mixer_call(x, mod, npre[1], npost[1], w_mix_in[layer].astype(BF16),
                        w_mix_out[layer].astype(BF16), cos_t, sin_t, attn_sinks[layer],
                        hgrn_lb_logits, hgrn_gnorm[layer][None, :], tm=tm_mix, layer=layer)
        x = _ffn_call(x, mod, npre[2], npost[2], ffn_w_in[layer, 1].astype(BF16),
                      ffn_w_out[layer, 1].astype(BF16), sub=2, tm=tm_ffn)
    return x
```

```python
import functools

import jax
import jax.numpy as jnp
from jax import lax
from jax.experimental import pallas as pl
from jax.experimental.pallas import tpu as pltpu

F32 = jnp.float32
BF16 = jnp.bfloat16

EPS = 1e-6
NEG_INF = -1e30
ROPE_THETA = 10000.0
LOG2E = 1.4426950408889634

CHUNK = 64
ATTN_HEAD_DIM = 64
ATTN_Q_HEADS = 8
ATTN_KV_HEADS = 2
ATTN_GROUP = ATTN_Q_HEADS // ATTN_KV_HEADS
ATTN_WIDTH = ATTN_Q_HEADS * ATTN_HEAD_DIM
KV_WIDTH = ATTN_KV_HEADS * ATTN_HEAD_DIM
HGRN_HEADS = 4
HGRN_DIM = 128
HGRN_WIDTH = HGRN_HEADS * HGRN_DIM
N_MOD = 9
FFN_RES_WEIGHT = 0.5

OFF_AQ = 0
OFF_AKV = OFF_AQ + ATTN_WIDTH
OFF_HQ = OFF_AKV + 2 * KV_WIDTH
OFF_HF = OFF_HQ + HGRN_WIDTH
OFF_HI = OFF_HF + HGRN_WIDTH
OFF_HG = OFF_HI + HGRN_WIDTH

LANES = 128
SUBLANES = 8
GROUP_ROWS = 2 * CHUNK
BAND_CHUNKS = 4
BAND_ROWS = BAND_CHUNKS * CHUNK
FFN_COL_TILE = 256
FFN_SLABS = 1
ROPE_FREQS = ATTN_HEAD_DIM // 2
VMEM_LIMIT_BYTES = 56 * 1024 * 1024

FLAG_LANE = ATTN_HEAD_DIM + BAND_CHUNKS

DIRECT_SPAN = 4


def _sigmoid(v):
    return 1.0 / (1.0 + jnp.exp(-v))


def _silu(v):
    hv = 0.5 * v
    return hv + hv * jnp.tanh(hv)


def _rms(v):
    return v * lax.rsqrt(jnp.mean(v * v, axis=-1, keepdims=True) + EPS)


def _dot(a, b):
    return jnp.dot(a, b, preferred_element_type=F32)


def _dot_nt(a, b):
    return lax.dot_general(a, b, (((1,), (1,)), ((), ())), preferred_element_type=F32)


def _dot_tn(a, b):
    return lax.dot_general(a, b, (((0,), (0,)), ((), ())), preferred_element_type=F32)


def _mod_kernel(c_ref, w_ref, b_ref, o_ref):
    c = c_ref[...]
    ca = (c * _sigmoid(c)).astype(BF16)
    o_ref[...] = _dot(ca, w_ref[...].astype(BF16)) + b_ref[...]


def _mod_call(c, w, b):
    bsz, d = c.shape
    n = w.shape[1]
    tn = 1024
    return pl.pallas_call(
        _mod_kernel,
        grid=(n // tn,),
        in_specs=[
            pl.BlockSpec((bsz, d), lambda j: (0, 0)),
            pl.BlockSpec((d, tn), lambda j: (0, j)),
            pl.BlockSpec((1, tn), lambda j: (0, j)),
        ],
        out_specs=pl.BlockSpec((bsz, tn), lambda j: (0, j)),
        out_shape=jax.ShapeDtypeStruct((bsz, n), F32),
        compiler_params=pltpu.CompilerParams(dimension_semantics=("arbitrary",)),
        name="mod",
    )(c, w, b.reshape(1, n))


def _rope_kernel(pos_ref, inv_ref, cos_ref, sin_ref):
    tr = pos_ref.shape[0]
    ang = pos_ref[...].astype(F32) * inv_ref[...]
    group = lax.broadcasted_iota(jnp.int32, (tr, LANES), 1) // ROPE_FREQS
    sign = jnp.where(group % 2 == 0, -1.0, 1.0)
    n_grp = LANES // ROPE_FREQS
    for tab, out_ref, signed in ((jnp.cos(ang), cos_ref, False), (jnp.sin(ang), sin_ref, True)):
        rolled = [tab] + [pltpu.roll(tab, ROPE_FREQS * k, 1) for k in range(1, n_grp)]
        for a in range(n_grp):
            piece = rolled[(0 - a) % n_grp]
            for kk in range(1, n_grp):
                piece = jnp.where(group == kk, rolled[(kk - a) % n_grp], piece)
            if signed:
                piece = piece * sign
            out_ref[pl.ds(a, tr, stride=n_grp), :] = piece


def _rope_call(pos_rep, inv_row):
    rows = pos_rep.shape[0]
    n_grp = LANES // ROPE_FREQS
    tr = min(rows, 512)
    out_spec = pl.BlockSpec((n_grp * tr, LANES), lambda i: (i, 0))
    return pl.pallas_call(
        _rope_kernel,
        grid=(rows // tr,),
        in_specs=[pl.BlockSpec((tr, LANES), lambda i: (i, 0)),
                  pl.BlockSpec((1, LANES), lambda i: (0, 0))],
        out_specs=[out_spec, out_spec],
        out_shape=[jax.ShapeDtypeStruct((n_grp * rows, LANES), F32)] * 2,
        compiler_params=pltpu.CompilerParams(dimension_semantics=("arbitrary",)),
        name="rope",
    )(pos_rep, inv_row)


def _ffn_kernel(x_ref, mod_ref, npre_ref, npost_ref, win_ref, wout_ref, o_ref, *, sub, d_ff, n_slabs):
    sh = mod_ref[0, 3 * sub:3 * sub + 1, :]
    sc = mod_ref[0, 3 * sub + 1:3 * sub + 2, :]
    gt = mod_ref[0, 3 * sub + 2:3 * sub + 3, :]
    tm = x_ref.shape[1]
    slab = tm // n_slabs
    for i in range(n_slabs):
        rows = slice(i * slab, (i + 1) * slab)
        x = x_ref[0, rows, :]
        h = (_rms(x) * npre_ref[...] * (1.0 + sc) + sh).astype(BF16)
        acc = jnp.zeros(x.shape, F32)
        for j in range(d_ff // FFN_COL_TILE):
            lo = j * FFN_COL_TILE
            g = _dot(h, win_ref[0, :, lo:lo + FFN_COL_TILE])
            u = _dot(h, win_ref[0, :, d_ff + lo:d_ff + lo + FFN_COL_TILE])
            a = (_silu(g) * u).astype(BF16)
            acc = acc + _dot(a, wout_ref[0, lo:lo + FFN_COL_TILE, :])
        o_ref[0, rows, :] = x + (FFN_RES_WEIGHT * gt) * (_rms(acc) * npost_ref[...])


def _resident(shape):
    return pl.BlockSpec(shape, lambda b, s: (0,) * len(shape), pipeline_mode=pl.Buffered(1))


def _ffn_call(x, mod, npre, npost, w_in, w_out, *, which, tm, n_slabs):
    bsz, seq, d = x.shape
    d_ff = w_out.shape[1]
    sub = 2 * which
    xspec = pl.BlockSpec((1, tm, d), lambda b, s: (b, s, 0))

    def picked(shape):
        return pl.BlockSpec((1,) + shape, lambda b, s: (which, 0, 0), pipeline_mode=pl.Buffered(1))

    return pl.pallas_call(
        functools.partial(_ffn_kernel, sub=sub, d_ff=d_ff, n_slabs=n_slabs),
        grid=(bsz, seq // tm),
        in_specs=[
            xspec,
            pl.BlockSpec((1, N_MOD, d), lambda b, s: (b, 0, 0)),
            _resident((1, d)),
            _resident((1, d)),
            picked((d, 2 * d_ff)),
            picked((d_ff, d)),
        ],
        out_specs=xspec,
        out_shape=jax.ShapeDtypeStruct(x.shape, F32),
        compiler_params=pltpu.CompilerParams(
            dimension_semantics=("arbitrary", "arbitrary"),
            vmem_limit_bytes=VMEM_LIMIT_BYTES),
        name="ffn%d" % sub,
    )(x, mod, npre, npost, w_in, w_out)


def _mixer_kernel(sinks_ref, x_ref, mod_ref, npre_ref, npost_ref, win_ref, wout_ref,
                  cos_ref, sin_ref, lbl_ref, gn_ref, tril_ref, o_ref,
                  kbuf, vbuf, st_ref, mix_scr, b_scr, *, tm, layer):
    si = pl.program_id(1)
    n_groups = tm // GROUP_ROWS

    lane = lax.broadcasted_iota(jnp.int32, (tm, LANES), 1)
    row = lax.broadcasted_iota(jnp.int32, (tm, LANES), 0)
    low_head = lane < ATTN_HEAD_DIM

    @pl.when(si == 0)
    def _():
        carry_lane = lax.broadcasted_iota(jnp.int32, (GROUP_ROWS, LANES), 1)
        k0 = jnp.where(carry_lane == FLAG_LANE, 1.0, 0.0).astype(BF16)
        v0 = jnp.where(carry_lane < ATTN_HEAD_DIM, 0.0, 1.0).astype(BF16)
        for hk in range(ATTN_KV_HEADS):
            kbuf[hk, 0:GROUP_ROWS, :] = k0
            vbuf[hk, 0:GROUP_ROWS, :] = v0
        st_ref[...] = jnp.zeros(st_ref.shape, F32)

    x = x_ref[0]
    sh = mod_ref[0, 3:4, :]
    sc = mod_ref[0, 4:5, :]
    gt = mod_ref[0, 5:6, :]
    h = (_rms(x) * npre_ref[...] * (1.0 + sc) + sh).astype(BF16)

    cos = cos_ref[0]
    sin = sin_ref[0]
    first_half = (lane % ATTN_HEAD_DIM) < (ATTN_HEAD_DIM // 2)
    chunk_id = (row // CHUNK) % BAND_CHUNKS
    key_extra = jnp.where(lane == ATTN_HEAD_DIM + chunk_id, 1.0, 0.0)
    hidden = ATTN_HEAD_DIM + (chunk_id + 1) % BAND_CHUNKS
    query_extra = jnp.where((lane == hidden) | (lane == FLAG_LANE), NEG_INF, 0.0)

    def rope(t):
        rot = jnp.where(first_half, pltpu.roll(t, LANES - 32, 1), pltpu.roll(t, 32, 1))
        return t * cos + rot * sin

    aq = _dot(h, win_ref[:, OFF_AQ:OFF_AQ + ATTN_WIDTH])
    akv = _dot(h, win_ref[:, OFF_AKV:OFF_AKV + 2 * KV_WIDTH])
    k = rope(akv[:, 0:KV_WIDTH])
    v = akv[:, KV_WIDTH:2 * KV_WIDTH]
    for src, buf, extra in ((k, kbuf, key_extra), (v, vbuf, 1.0)):
        buf[0, GROUP_ROWS:GROUP_ROWS + tm, :] = jnp.where(low_head, src, extra).astype(BF16)
        buf[1, GROUP_ROWS:GROUP_ROWS + tm, :] = jnp.where(
            low_head, pltpu.roll(src, ATTN_HEAD_DIM, 1), extra).astype(BF16)

    qh = []
    for jb in range(ATTN_Q_HEADS // 2):
        qb = rope(aq[:, jb * LANES:(jb + 1) * LANES]) * (ATTN_HEAD_DIM ** -0.5 * LOG2E)
        qh.append(jnp.where(low_head, qb, query_extra).astype(BF16))
        qh.append(jnp.where(low_head, pltpu.roll(qb, ATTN_HEAD_DIM, 1), query_extra).astype(BF16))

    low_grp = lax.broadcasted_iota(jnp.int32, (GROUP_ROWS, LANES), 1) < ATTN_HEAD_DIM
    def attn_block(g, hk):
        r0 = g * GROUP_ROWS
        heads = [hk * ATTN_GROUP + i for i in range(ATTN_GROUP)]
        qs = jnp.concatenate([qh[hd][r0:r0 + GROUP_ROWS, :] for hd in heads], axis=0)
        kband = kbuf[hk, r0:r0 + BAND_ROWS, :]
        vband = vbuf[hk, r0:r0 + BAND_ROWS, :]
        s = _dot_nt(qs, kband)
        m = jnp.max(s, axis=-1, keepdims=True)
        p = jnp.exp2(s - m)
        o = _dot(p.astype(BF16), vband)
        for pair in range(ATTN_GROUP // 2):
            res = []
            for odd in range(2):
                a0 = (2 * pair + odd) * GROUP_ROWS
                o_h = o[a0:a0 + GROUP_ROWS, :]
                sink_term = jnp.exp2(sinks_ref[heads[2 * pair + odd]] * LOG2E
                                     - m[a0:a0 + GROUP_ROWS, :])
                swapped = pltpu.roll(o_h, ATTN_HEAD_DIM, 1)
                if odd:
                    res.append(swapped / (o_h + sink_term))
                else:
                    res.append(o_h / (swapped + sink_term))
            col = (hk * (ATTN_GROUP // 2) + pair) * LANES
            mix_scr[r0:r0 + GROUP_ROWS, col:col + LANES] = (
                jnp.where(low_grp, res[0], res[1]).astype(BF16))

    lbl = lbl_ref[...]
    e_lb = jnp.exp(lbl - jnp.max(lbl, axis=0, keepdims=True))
    lb_all = jnp.sum(e_lb[0:layer + 1, :], axis=0, keepdims=True) / jnp.sum(e_lb, axis=0, keepdims=True)

    hq = _dot(h, win_ref[:, OFF_HQ:OFF_HQ + HGRN_WIDTH])
    hf = _dot(h, win_ref[:, OFF_HF:OFF_HF + HGRN_WIDTH])
    hi = _dot(h, win_ref[:, OFF_HI:OFF_HI + HGRN_WIDTH])
    hg = _dot(h, win_ref[:, OFF_HG:OFF_HG + HGRN_WIDTH])

    f = lb_all + (1.0 - lb_all) * _sigmoid(hf)
    g = jnp.log2(f)
    g1 = g.astype(BF16)
    r1 = g - g1.astype(F32)
    g2 = r1.astype(BF16)
    g3 = (r1 - g2.astype(F32)).astype(BF16)
    tril = tril_ref[...]
    b_all = _dot(tril, g1) + _dot(tril, g2) + _dot(tril, g3)
    b_scr[...] = b_all

    half_tile = tm // 2
    level_halves = []
    span = half_tile
    while span >= DIRECT_SPAN:
        level_halves.append(span)
        span //= 2
    ti = lax.broadcasted_iota(jnp.int32, (half_tile, half_tile), 0)
    sj = lax.broadcasted_iota(jnp.int32, (half_tile, half_tile), 1)
    quad_mask = {}
    for half in level_halves[1:]:
        blk = 2 * half
        quad_mask[half] = ((ti // blk) == (sj // blk)) & ((ti % blk) >= half) & ((sj % blk) < half)
    diag = [sj == ti - delta for delta in range(DIRECT_SPAN)]
    sub_row = row % SUBLANES
    direct_ok = [(sub_row % DIRECT_SPAN) >= delta for delta in range(DIRECT_SPAN)]
    later4 = (sub_row % (2 * DIRECT_SPAN)) >= DIRECT_SPAN

    gn = gn_ref[...]

    def hgrn_head(hd):
        cols = slice(hd * HGRN_DIM, (hd + 1) * HGRN_DIM)
        rq = _silu(hq[:, cols]) * (HGRN_DIM ** -0.5)
        rk = 1.0 - f[:, cols]
        rv = hi[:, cols].astype(BF16)
        gate = _silu(hg[:, cols])
        b = b_all[:, cols]

        lvl_w = {}
        for half in level_halves:
            blk = 2 * half
            expo, operand = [], []
            for mblk in range(tm // blk):
                lo = mblk * blk
                ref_row = b_scr[pl.ds(lo + half - 1, 1), cols]
                if half >= SUBLANES:
                    ref_half = jnp.broadcast_to(ref_row, (half, HGRN_DIM))
                    expo += [ref_half - b[lo:lo + half, :], b[lo + half:lo + blk, :] - ref_half]
                    operand += [rk[lo:lo + half, :], rq[lo + half:lo + blk, :]]
                else:
                    expo.append(jnp.broadcast_to(ref_row, (blk, HGRN_DIM)))
            if half >= SUBLANES:
                w = jnp.concatenate(operand, axis=0) * jnp.exp2(jnp.concatenate(expo, axis=0))
            else:
                dist = jnp.abs(b - jnp.concatenate(expo, axis=0))
                w = jnp.where(later4, rq, rk) * jnp.exp2(-dist)
            lvl_w[half] = w.astype(BF16)

        def shifted(t, delta):
            t3 = t.reshape(tm // SUBLANES, SUBLANES, HGRN_DIM)
            return pltpu.roll(t3, delta, 1).reshape(tm, HGRN_DIM)

        direct = []
        for delta in range(DIRECT_SPAN):
            if delta == 0:
                prod = rq * rk
            else:
                ok = direct_ok[delta]
                decay = jnp.exp2(jnp.where(ok, b - shifted(b, delta), 0.0))
                prod = jnp.where(ok, rq * shifted(rk, delta) * decay, 0.0)
            direct.append(jnp.sum(prod, axis=-1, keepdims=True))

        top = _dot_nt(lvl_w[half_tile][half_tile:, :], lvl_w[half_tile][0:half_tile, :])
        quads = []
        for qd in range(2):
            o0 = qd * half_tile
            sc_q = jnp.zeros((half_tile, half_tile), F32)
            for delta in range(DIRECT_SPAN):
                sc_q = jnp.where(diag[delta], direct[delta][o0:o0 + half_tile, :], sc_q)
            for half in level_halves[1:]:
                w_q = lvl_w[half][o0:o0 + half_tile, :]
                sc_q = jnp.where(quad_mask[half], _dot_nt(w_q, w_q), sc_q)
            quads.append(sc_q.astype(BF16))
        s_hi = jnp.concatenate([top.astype(BF16), quads[1]], axis=1)

        st = st_ref[hd]
        b_last = b[tm - 1:tm, :]
        q_in = (rq * jnp.exp2(b)).astype(BF16)
        k_out = (rk * jnp.exp2(b_last - b)).astype(BF16)
        o = _dot_nt(q_in, st.astype(BF16)) + jnp.concatenate(
            [_dot(quads[0], rv[0:half_tile, :]), _dot(s_hi, rv)], axis=0)
        st_ref[hd] = st * jnp.exp2(b_last) + _dot_tn(rv, k_out)
        o = _rms(o) * gn * gate
        mix_scr[:, ATTN_WIDTH + hd * HGRN_DIM:ATTN_WIDTH + (hd + 1) * HGRN_DIM] = o.astype(BF16)

    attn_blocks = [(g, hk) for g in range(n_groups) for hk in range(ATTN_KV_HEADS)]
    for i in range(max(len(attn_blocks), HGRN_HEADS)):
        if i < len(attn_blocks):
            attn_block(*attn_blocks[i])
        if i < HGRN_HEADS:
            hgrn_head(i)

    for hk in range(ATTN_KV_HEADS):
        kbuf[hk, 0:GROUP_ROWS, :] = kbuf[hk, tm:tm + GROUP_ROWS, :]
        vbuf[hk, 0:GROUP_ROWS, :] = vbuf[hk, tm:tm + GROUP_ROWS, :]

    y = _dot(mix_scr[...], wout_ref[...])
    o_ref[0] = x + gt * (_rms(y) * npost_ref[...])


def _mixer_call(x, mod, npre, npost, w_in, w_out, cos_t, sin_t, sinks, lb_logits, gnorm, *, tm, layer):
    bsz, seq, d = x.shape
    d_proj = w_in.shape[1]
    d_mix = w_out.shape[0]
    assert tm % (BAND_CHUNKS * CHUNK) == 0
    r = jnp.arange(tm)
    tril = (r[None, :] <= r[:, None]).astype(BF16)
    xspec = pl.BlockSpec((1, tm, d), lambda b, s: (b, s, 0))
    tspec = pl.BlockSpec((1, tm, LANES), lambda b, s: (b, s, 0))
    return pl.pallas_call(
        functools.partial(_mixer_kernel, tm=tm, layer=layer),
        grid=(bsz, seq // tm),
        in_specs=[
            pl.BlockSpec(memory_space=pltpu.SMEM),
            xspec,
            pl.BlockSpec((1, N_MOD, d), lambda b, s: (b, 0, 0)),
            _resident((1, d)),
            _resident((1, d)),
            _resident((d, d_proj)),
            _resident((d_mix, d)),
            tspec,
            tspec,
            _resident(lb_logits.shape),
            _resident((1, HGRN_DIM)),
            _resident((tm, tm)),
        ],
        out_specs=xspec,
        out_shape=jax.ShapeDtypeStruct(x.shape, F32),
        scratch_shapes=[
            pltpu.VMEM((ATTN_KV_HEADS, GROUP_ROWS + tm, LANES), BF16),
            pltpu.VMEM((ATTN_KV_HEADS, GROUP_ROWS + tm, LANES), BF16),
            pltpu.VMEM((HGRN_HEADS, HGRN_DIM, HGRN_DIM), F32),
            pltpu.VMEM((tm, d_mix), BF16),
            pltpu.VMEM((tm, HGRN_WIDTH), F32),
        ],
        compiler_params=pltpu.CompilerParams(
            dimension_semantics=("arbitrary", "arbitrary"),
            vmem_limit_bytes=VMEM_LIMIT_BYTES),
        name="mixer",
    )(sinks, x, mod, npre, npost, w_in, w_out, cos_t, sin_t, lb_logits, gnorm, tril)


def kernel(x, c, positions, w_cond, b_cond, norm_pre, norm_post, ffn_w_in, ffn_w_out,
           w_mix_in, w_mix_out, attn_sinks, hgrn_lb_logits, hgrn_gnorm):
    bsz, seq, d = x.shape
    depth = w_cond.shape[0]
    tm_ffn = min(seq, 512)
    tm_mix = min(seq, 256)

    inv_freq = 1.0 / (ROPE_THETA ** (jnp.arange(0, ATTN_HEAD_DIM, 2, dtype=F32) / ATTN_HEAD_DIM))
    inv_row = jnp.tile(inv_freq, LANES // ROPE_FREQS)[None, :]
    pos_rep = jnp.broadcast_to(positions.reshape(-1, 1), (bsz * seq, ROPE_FREQS)).reshape(-1, LANES)
    cos_t, sin_t = _rope_call(pos_rep, inv_row)
    cos_t = cos_t.reshape(bsz, seq, LANES)
    sin_t = sin_t.reshape(bsz, seq, LANES)

    w_ffn_in = ffn_w_in.astype(BF16)
    w_ffn_out = ffn_w_out.astype(BF16)
    for layer in range(depth):
        mod = _mod_call(c, w_cond[layer], b_cond[layer]).reshape(bsz, N_MOD, d)
        npre = norm_pre[layer][:, None, :]
        npost = norm_post[layer][:, None, :]
        x = _ffn_call(x, mod, npre[0], npost[0], w_ffn_in[layer], w_ffn_out[layer],
                      which=0, tm=tm_ffn, n_slabs=FFN_SLABS)
        x = _mixer_call(x, mod, npre[1], npost[1], w_mix_in[layer].astype(BF16),
                        w_mix_out[layer].astype(BF16), cos_t, sin_t, attn_sinks[layer],
                        hgrn_lb_logits, hgrn_gnorm[layer][None, :], tm=tm_mix, layer=layer)
        x = _ffn_call(x, mod, npre[2], npost[2], w_ffn_in[layer], w_ffn_out[layer],
                      which=1, tm=tm_ffn, n_slabs=FFN_SLABS)
    return x
```

```python
import functools

import jax
import jax.numpy as jnp
from jax import lax
from jax.experimental import pallas as pl
from jax.experimental.pallas import tpu as pltpu

F32 = jnp.float32
BF16 = jnp.bfloat16

EPS = 1e-6
NEG_INF = -1e30
ROPE_THETA = 10000.0
LOG2E = 1.4426950408889634

CHUNK = 64
ATTN_HEAD_DIM = 64
ATTN_Q_HEADS = 8
ATTN_KV_HEADS = 2
ATTN_GROUP = ATTN_Q_HEADS // ATTN_KV_HEADS
ATTN_WIDTH = ATTN_Q_HEADS * ATTN_HEAD_DIM
KV_WIDTH = ATTN_KV_HEADS * ATTN_HEAD_DIM
HGRN_HEADS = 4
HGRN_DIM = 128
HGRN_WIDTH = HGRN_HEADS * HGRN_DIM
N_MOD = 9
FFN_RES_WEIGHT = 0.5

OFF_AQ = 0
OFF_AKV = OFF_AQ + ATTN_WIDTH
OFF_HQ = OFF_AKV + 2 * KV_WIDTH
OFF_HF = OFF_HQ + HGRN_WIDTH
OFF_HI = OFF_HF + HGRN_WIDTH
OFF_HG = OFF_HI + HGRN_WIDTH

LANES = 128
SUBLANES = 8
GROUP_ROWS = 2 * CHUNK
BAND_CHUNKS = 4
BAND_ROWS = BAND_CHUNKS * CHUNK
FFN_COL_TILE = 256
ROPE_FREQS = ATTN_HEAD_DIM // 2
HGRN_SPAN = 256
VMEM_LIMIT_BYTES = 56 * 1024 * 1024

FLAG_LANE = ATTN_HEAD_DIM + BAND_CHUNKS

DIRECT_SPAN = 4


def _sigmoid(v):
    return 1.0 / (1.0 + jnp.exp(-v))


def _silu(v):
    hv = 0.5 * v
    return hv + hv * jnp.tanh(hv)


def _rms(v):
    return v * lax.rsqrt(jnp.mean(v * v, axis=-1, keepdims=True) + EPS)


def _dot(a, b):
    return jnp.dot(a, b, preferred_element_type=F32)


def _dot_nt(a, b):
    return lax.dot_general(a, b, (((1,), (1,)), ((), ())), preferred_element_type=F32)


def _dot_tn(a, b):
    return lax.dot_general(a, b, (((0,), (0,)), ((), ())), preferred_element_type=F32)


def _mod_kernel(c_ref, w_ref, b_ref, o_ref):
    c = c_ref[...]
    ca = (c * _sigmoid(c)).astype(BF16)
    o_ref[...] = _dot(ca, w_ref[...].astype(BF16)) + b_ref[...]


def _mod_call(c, w, b):
    bsz, d = c.shape
    n = w.shape[1]
    tn = 1024
    return pl.pallas_call(
        _mod_kernel,
        grid=(n // tn,),
        in_specs=[
            pl.BlockSpec((bsz, d), lambda j: (0, 0)),
            pl.BlockSpec((d, tn), lambda j: (0, j)),
            pl.BlockSpec((1, tn), lambda j: (0, j)),
        ],
        out_specs=pl.BlockSpec((bsz, tn), lambda j: (0, j)),
        out_shape=jax.ShapeDtypeStruct((bsz, n), F32),
        compiler_params=pltpu.CompilerParams(dimension_semantics=("arbitrary",)),
        name="mod",
    )(c, w, b.reshape(1, n))


def _rope_kernel(pos_ref, inv_ref, cos_ref, sin_ref):
    tr = pos_ref.shape[0]
    ang = pos_ref[...].astype(F32) * inv_ref[...]
    group = lax.broadcasted_iota(jnp.int32, (tr, LANES), 1) // ROPE_FREQS
    sign = jnp.where(group % 2 == 0, -1.0, 1.0)
    n_grp = LANES // ROPE_FREQS
    for tab, out_ref, signed in ((jnp.cos(ang), cos_ref, False), (jnp.sin(ang), sin_ref, True)):
        rolled = [tab] + [pltpu.roll(tab, ROPE_FREQS * k, 1) for k in range(1, n_grp)]
        for a in range(n_grp):
            piece = rolled[(0 - a) % n_grp]
            for kk in range(1, n_grp):
                piece = jnp.where(group == kk, rolled[(kk - a) % n_grp], piece)
            if signed:
                piece = piece * sign
            out_ref[pl.ds(a, tr, stride=n_grp), :] = piece


def _rope_call(pos_rep, inv_row):
    rows = pos_rep.shape[0]
    n_grp = LANES // ROPE_FREQS
    tr = min(rows, 512)
    out_spec = pl.BlockSpec((n_grp * tr, LANES), lambda i: (i, 0))
    return pl.pallas_call(
        _rope_kernel,
        grid=(rows // tr,),
        in_specs=[pl.BlockSpec((tr, LANES), lambda i: (i, 0)),
                  pl.BlockSpec((1, LANES), lambda i: (0, 0))],
        out_specs=[out_spec, out_spec],
        out_shape=[jax.ShapeDtypeStruct((n_grp * rows, LANES), F32)] * 2,
        compiler_params=pltpu.CompilerParams(dimension_semantics=("arbitrary",)),
        name="rope",
    )(pos_rep, inv_row)


def _resident(shape, index=None):
    index = (0,) * len(shape) if index is None else index
    return pl.BlockSpec(shape, lambda b, s: index, pipeline_mode=pl.Buffered(1))


def _ffn_kernel(x_ref, mod_ref, npre_ref, npost_ref, win_ref, wout_ref, o_ref, *, sub, d_ff):
    sh = mod_ref[0, 3 * sub:3 * sub + 1, :]
    sc = mod_ref[0, 3 * sub + 1:3 * sub + 2, :]
    gt = mod_ref[0, 3 * sub + 2:3 * sub + 3, :]
    x = x_ref[0]
    h = (_rms(x) * (npre_ref[...] * (1.0 + sc)) + sh).astype(BF16)
    acc = jnp.zeros(x.shape, F32)
    for j in range(d_ff // FFN_COL_TILE):
        lo = j * FFN_COL_TILE
        g = _dot(h, win_ref[0, :, lo:lo + FFN_COL_TILE])
        u = _dot(h, win_ref[0, :, d_ff + lo:d_ff + lo + FFN_COL_TILE])
        a = (_silu(g) * u).astype(BF16)
        acc = acc + _dot(a, wout_ref[0, lo:lo + FFN_COL_TILE, :])
    o_ref[0] = x + _rms(acc) * ((FFN_RES_WEIGHT * gt) * npost_ref[...])


def _ffn_call(x, mod, npre, npost, w_in, w_out, *, which, tm):
    bsz, seq, d = x.shape
    d_ff = w_out.shape[1]
    sub = 2 * which
    xspec = pl.BlockSpec((1, tm, d), lambda b, s: (b, s, 0))
    return pl.pallas_call(
        functools.partial(_ffn_kernel, sub=sub, d_ff=d_ff),
        grid=(bsz, seq // tm),
        in_specs=[
            xspec,
            pl.BlockSpec((1, N_MOD, d), lambda b, s: (b, 0, 0)),
            _resident((1, d)),
            _resident((1, d)),
            _resident((1, d, 2 * d_ff), (which, 0, 0)),
            _resident((1, d_ff, d), (which, 0, 0)),
        ],
        out_specs=xspec,
        out_shape=jax.ShapeDtypeStruct(x.shape, F32),
        compiler_params=pltpu.CompilerParams(
            dimension_semantics=("arbitrary", "arbitrary"),
            vmem_limit_bytes=VMEM_LIMIT_BYTES),
        name="ffn%d" % sub,
    )(x, mod, npre, npost, w_in, w_out)


def _mixer_kernel(sinks_ref, x_ref, mod_ref, npre_ref, npost_ref, win_ref, wout_ref,
                  cos_ref, sin_ref, lbl_ref, gn_ref, tril_ref, o_ref,
                  kbuf, vbuf, st_ref, mix_scr, b_scr, *, tm, layer):
    si = pl.program_id(1)
    n_groups = tm // GROUP_ROWS
    n_spans = tm // HGRN_SPAN

    lane = lax.broadcasted_iota(jnp.int32, (tm, LANES), 1)
    row = lax.broadcasted_iota(jnp.int32, (tm, LANES), 0)
    low_head = lane < ATTN_HEAD_DIM

    @pl.when(si == 0)
    def _():
        carry_lane = lax.broadcasted_iota(jnp.int32, (GROUP_ROWS, LANES), 1)
        k0 = jnp.where(carry_lane == FLAG_LANE, 1.0, 0.0).astype(BF16)
        v0 = jnp.where(carry_lane < ATTN_HEAD_DIM, 0.0, 1.0).astype(BF16)
        for hk in range(ATTN_KV_HEADS):
            kbuf[hk, 0:GROUP_ROWS, :] = k0
            vbuf[hk, 0:GROUP_ROWS, :] = v0
        st_ref[...] = jnp.zeros(st_ref.shape, F32)

    x = x_ref[0]
    sh = mod_ref[0, 3:4, :]
    sc = mod_ref[0, 4:5, :]
    gt = mod_ref[0, 5:6, :]
    h = (_rms(x) * (npre_ref[...] * (1.0 + sc)) + sh).astype(BF16)

    cos = cos_ref[0]
    sin = sin_ref[0]
    first_half = (lane % ATTN_HEAD_DIM) < (ATTN_HEAD_DIM // 2)
    chunk_id = (row // CHUNK) % BAND_CHUNKS
    key_extra = jnp.where(lane == ATTN_HEAD_DIM + chunk_id, 1.0, 0.0)
    hidden = ATTN_HEAD_DIM + (chunk_id + 1) % BAND_CHUNKS
    query_extra = jnp.where((lane == hidden) | (lane == FLAG_LANE), NEG_INF, 0.0)

    def rope(t):
        rot = jnp.where(first_half, pltpu.roll(t, LANES - 32, 1), pltpu.roll(t, 32, 1))
        return t * cos + rot * sin

    aq = _dot(h, win_ref[:, OFF_AQ:OFF_AQ + ATTN_WIDTH])
    akv = _dot(h, win_ref[:, OFF_AKV:OFF_AKV + 2 * KV_WIDTH])
    k = rope(akv[:, 0:KV_WIDTH])
    v = akv[:, KV_WIDTH:2 * KV_WIDTH]
    for src, buf, extra in ((k, kbuf, key_extra), (v, vbuf, 1.0)):
        buf[0, GROUP_ROWS:GROUP_ROWS + tm, :] = jnp.where(low_head, src, extra).astype(BF16)
        buf[1, GROUP_ROWS:GROUP_ROWS + tm, :] = jnp.where(
            low_head, pltpu.roll(src, ATTN_HEAD_DIM, 1), extra).astype(BF16)

    qh = []
    for jb in range(ATTN_Q_HEADS // 2):
        qb = rope(aq[:, jb * LANES:(jb + 1) * LANES]) * (ATTN_HEAD_DIM ** -0.5 * LOG2E)
        qh.append(jnp.where(low_head, qb, query_extra).astype(BF16))
        qh.append(jnp.where(low_head, pltpu.roll(qb, ATTN_HEAD_DIM, 1), query_extra).astype(BF16))

    low_grp = lax.broadcasted_iota(jnp.int32, (GROUP_ROWS, LANES), 1) < ATTN_HEAD_DIM

    def attn_block(g, hk):
        r0 = g * GROUP_ROWS
        heads = [hk * ATTN_GROUP + i for i in range(ATTN_GROUP)]
        qs = jnp.concatenate([qh[hd][r0:r0 + GROUP_ROWS, :] for hd in heads], axis=0)
        kband = kbuf[hk, r0:r0 + BAND_ROWS, :]
        vband = vbuf[hk, r0:r0 + BAND_ROWS, :]
        s = _dot_nt(qs, kband)
        m = jnp.max(s, axis=-1, keepdims=True)
        p = jnp.exp2(s - m)
        o = _dot(p.astype(BF16), vband)
        for pair in range(ATTN_GROUP // 2):
            res = []
            for odd in range(2):
                a0 = (2 * pair + odd) * GROUP_ROWS
                o_h = o[a0:a0 + GROUP_ROWS, :]
                sink_term = jnp.exp2(sinks_ref[heads[2 * pair + odd]] * LOG2E
                                     - m[a0:a0 + GROUP_ROWS, :])
                swapped = pltpu.roll(o_h, ATTN_HEAD_DIM, 1)
                if odd:
                    res.append(swapped / (o_h + sink_term))
                else:
                    res.append(o_h / (swapped + sink_term))
            col = (hk * (ATTN_GROUP // 2) + pair) * LANES
            mix_scr[r0:r0 + GROUP_ROWS, col:col + LANES] = (
                jnp.where(low_grp, res[0], res[1]).astype(BF16))

    lbl = lbl_ref[...]
    e_lb = jnp.exp(lbl - jnp.max(lbl, axis=0, keepdims=True))
    lb_all = jnp.sum(e_lb[0:layer + 1, :], axis=0, keepdims=True) / jnp.sum(e_lb, axis=0, keepdims=True)

    hq = _dot(h, win_ref[:, OFF_HQ:OFF_HQ + HGRN_WIDTH])
    hf = _dot(h, win_ref[:, OFF_HF:OFF_HF + HGRN_WIDTH])
    hi = _dot(h, win_ref[:, OFF_HI:OFF_HI + HGRN_WIDTH])
    hg = _dot(h, win_ref[:, OFF_HG:OFF_HG + HGRN_WIDTH])

    half_open = 0.5 * (1.0 - lb_all)
    swing = half_open * jnp.tanh(0.5 * hf)
    f = (lb_all + half_open) + swing
    one_minus_f = half_open - swing
    g = jnp.log2(f)
    g1 = g.astype(BF16)
    g2 = (g - g1.astype(F32)).astype(BF16)
    tril = tril_ref[...]
    for sp in range(n_spans):
        rows = slice(sp * HGRN_SPAN, (sp + 1) * HGRN_SPAN)
        b_scr[rows, :] = _dot(tril, g1[rows, :]) + _dot(tril, g2[rows, :])

    half_span = HGRN_SPAN // 2
    level_halves = []
    width = half_span
    while width >= DIRECT_SPAN:
        level_halves.append(width)
        width //= 2
    ti = lax.broadcasted_iota(jnp.int32, (half_span, half_span), 0)
    sj = lax.broadcasted_iota(jnp.int32, (half_span, half_span), 1)
    quad_mask = {}
    for half in level_halves[1:]:
        blk = 2 * half
        quad_mask[half] = ((ti // blk) == (sj // blk)) & ((ti % blk) >= half) & ((sj % blk) < half)
    diag = [sj == ti - delta for delta in range(DIRECT_SPAN)]
    sub_row = lax.broadcasted_iota(jnp.int32, (HGRN_SPAN, LANES), 0) % SUBLANES
    direct_ok = [(sub_row % DIRECT_SPAN) >= delta for delta in range(DIRECT_SPAN)]
    later4 = (sub_row % (2 * DIRECT_SPAN)) >= DIRECT_SPAN

    gn = gn_ref[...]

    def hgrn_unit(hd, sp):
        cols = slice(hd * HGRN_DIM, (hd + 1) * HGRN_DIM)
        base = sp * HGRN_SPAN
        rows = slice(base, base + HGRN_SPAN)
        rq = _silu(hq[rows, cols]) * (HGRN_DIM ** -0.5)
        rk = one_minus_f[rows, cols]
        rv = hi[rows, cols].astype(BF16)
        gate = _silu(hg[rows, cols])
        b = b_scr[rows, cols]

        lvl_w = {}
        for half in level_halves:
            blk = 2 * half
            expo, operand = [], []
            for mblk in range(HGRN_SPAN // blk):
                lo = mblk * blk
                ref_row = b_scr[pl.ds(base + lo + half - 1, 1), cols]
                if half >= SUBLANES:
                    ref_half = jnp.broadcast_to(ref_row, (half, HGRN_DIM))
                    expo += [ref_half - b[lo:lo + half, :], b[lo + half:lo + blk, :] - ref_half]
                    operand += [rk[lo:lo + half, :], rq[lo + half:lo + blk, :]]
                else:
                    expo.append(jnp.broadcast_to(ref_row, (blk, HGRN_DIM)))
            if half >= SUBLANES:
                w = jnp.concatenate(operand, axis=0) * jnp.exp2(jnp.concatenate(expo, axis=0))
            else:
                dist = jnp.abs(b - jnp.concatenate(expo, axis=0))
                w = jnp.where(later4, rq, rk) * jnp.exp2(-dist)
            lvl_w[half] = w.astype(BF16)

        def shifted(t, delta):
            t3 = t.reshape(HGRN_SPAN // SUBLANES, SUBLANES, HGRN_DIM)
            return pltpu.roll(t3, delta, 1).reshape(HGRN_SPAN, HGRN_DIM)

        direct = []
        for delta in range(DIRECT_SPAN):
            if delta == 0:
                prod = rq * rk
            else:
                ok = direct_ok[delta]
                decay = jnp.exp2(jnp.where(ok, b - shifted(b, delta), 0.0))
                prod = jnp.where(ok, rq * shifted(rk, delta) * decay, 0.0)
            direct.append(jnp.sum(prod, axis=-1, keepdims=True))

        top = _dot_nt(lvl_w[half_span][half_span:, :], lvl_w[half_span][0:half_span, :])
        quads = []
        for qd in range(2):
            o0 = qd * half_span
            sc_q = jnp.zeros((half_span, half_span), F32)
            for delta in range(DIRECT_SPAN):
                sc_q = jnp.where(diag[delta], direct[delta][o0:o0 + half_span, :], sc_q)
            for half in level_halves[1:]:
                w_q = lvl_w[half][o0:o0 + half_span, :]
                sc_q = jnp.where(quad_mask[half], _dot_nt(w_q, w_q), sc_q)
            quads.append(sc_q.astype(BF16))
        s_hi = jnp.concatenate([top.astype(BF16), quads[1]], axis=1)

        st = st_ref[hd]
        b_last = b[HGRN_SPAN - 1:HGRN_SPAN, :]
        q_in = (rq * jnp.exp2(b)).astype(BF16)
        k_out = (rk * jnp.exp2(b_last - b)).astype(BF16)
        o = _dot_nt(q_in, st.astype(BF16)) + jnp.concatenate(
            [_dot(quads[0], rv[0:half_span, :]), _dot(s_hi, rv)], axis=0)
        st_ref[hd] = st * jnp.exp2(b_last) + _dot_tn(rv, k_out)
        o = _rms(o) * gn * gate
        mix_scr[rows, ATTN_WIDTH + hd * HGRN_DIM:ATTN_WIDTH + (hd + 1) * HGRN_DIM] = o.astype(BF16)

    attn_blocks = [(g, hk) for g in range(n_groups) for hk in range(ATTN_KV_HEADS)]
    hgrn_units = [(hd, sp) for sp in range(n_spans) for hd in range(HGRN_HEADS)]
    for i in range(max(len(attn_blocks), len(hgrn_units))):
        if i < len(attn_blocks):
            attn_block(*attn_blocks[i])
        if i < len(hgrn_units):
            hgrn_unit(*hgrn_units[i])

    for hk in range(ATTN_KV_HEADS):
        kbuf[hk, 0:GROUP_ROWS, :] = kbuf[hk, tm:tm + GROUP_ROWS, :]
        vbuf[hk, 0:GROUP_ROWS, :] = vbuf[hk, tm:tm + GROUP_ROWS, :]

    y = _dot(mix_scr[...], wout_ref[...])
    o_ref[0] = x + _rms(y) * (gt * npost_ref[...])


def _mixer_call(x, mod, npre, npost, w_in, w_out, cos_t, sin_t, sinks, lb_logits, gnorm, *, tm, layer):
    bsz, seq, d = x.shape
    d_proj = w_in.shape[1]
    d_mix = w_out.shape[0]
    assert tm % (BAND_CHUNKS * CHUNK) == 0 and tm % HGRN_SPAN == 0
    r = jnp.arange(HGRN_SPAN)
    tril = (r[None, :] <= r[:, None]).astype(BF16)
    xspec = pl.BlockSpec((1, tm, d), lambda b, s: (b, s, 0))
    tspec = pl.BlockSpec((1, tm, LANES), lambda b, s: (b, s, 0))
    return pl.pallas_call(
        functools.partial(_mixer_kernel, tm=tm, layer=layer),
        grid=(bsz, seq // tm),
        in_specs=[
            pl.BlockSpec(memory_space=pltpu.SMEM),
            xspec,
            pl.BlockSpec((1, N_MOD, d), lambda b, s: (b, 0, 0)),
            _resident((1, d)),
            _resident((1, d)),
            _resident((d, d_proj)),
            _resident((d_mix, d)),
            tspec,
            tspec,
            _resident(lb_logits.shape),
            _resident((1, HGRN_DIM)),
            _resident((HGRN_SPAN, HGRN_SPAN)),
        ],
        out_specs=xspec,
        out_shape=jax.ShapeDtypeStruct(x.shape, F32),
        scratch_shapes=[
            pltpu.VMEM((ATTN_KV_HEADS, GROUP_ROWS + tm, LANES), BF16),
            pltpu.VMEM((ATTN_KV_HEADS, GROUP_ROWS + tm, LANES), BF16),
            pltpu.VMEM((HGRN_HEADS, HGRN_DIM, HGRN_DIM), F32),
            pltpu.VMEM((tm, d_mix), BF16),
            pltpu.VMEM((tm, HGRN_WIDTH), F32),
        ],
        compiler_params=pltpu.CompilerParams(
            dimension_semantics=("arbitrary", "arbitrary"),
            vmem_limit_bytes=VMEM_LIMIT_BYTES),
        name="mixer",
    )(sinks, x, mod, npre, npost, w_in, w_out, cos_t, sin_t, lb_logits, gnorm, tril)


def kernel(x, c, positions, w_cond, b_cond, norm_pre, norm_post, ffn_w_in, ffn_w_out,
           w_mix_in, w_mix_out, attn_sinks, hgrn_lb_logits, hgrn_gnorm):
    bsz, seq, d = x.shape
    depth = w_cond.shape[0]
    tm_ffn = min(seq, 512)
    tm_mix = min(seq, 512)

    inv_freq = 1.0 / (ROPE_THETA ** (jnp.arange(0, ATTN_HEAD_DIM, 2, dtype=F32) / ATTN_HEAD_DIM))
    inv_row = jnp.tile(inv_freq, LANES // ROPE_FREQS)[None, :]
    pos_rep = jnp.broadcast_to(positions.reshape(-1, 1), (bsz * seq, ROPE_FREQS)).reshape(-1, LANES)
    cos_t, sin_t = _rope_call(pos_rep, inv_row)
    cos_t = cos_t.reshape(bsz, seq, LANES)
    sin_t = sin_t.reshape(bsz, seq, LANES)

    w_ffn_in = ffn_w_in.astype(BF16)
    w_ffn_out = ffn_w_out.astype(BF16)
    for layer in range(depth):
        mod = _mod_call(c, w_cond[layer], b_cond[layer]).reshape(bsz, N_MOD, d)
        npre = norm_pre[layer][:, None, :]
        npost = norm_post[layer][:, None, :]
        x = _ffn_call(x, mod, npre[0], npost[0], w_ffn_in[layer], w_ffn_out[layer],
                      which=0, tm=tm_ffn)
        x = _mixer_call(x, mod, npre[1], npost[1], w_mix_in[layer].astype(BF16),
                        w_mix_out[layer].astype(BF16), cos_t, sin_t, attn_sinks[layer],
                        hgrn_lb_logits, hgrn_gnorm[layer][None, :], tm=tm_mix, layer=layer)
        x = _ffn_call(x, mod, npre[2], npost[2], w_ffn_in[layer], w_ffn_out[layer],
                      which=1, tm=tm_ffn)
    return x
```

```python
import functools

import jax
import jax.numpy as jnp
from jax import lax
from jax.experimental import pallas as pl
from jax.experimental.pallas import tpu as pltpu

F32 = jnp.float32
BF16 = jnp.bfloat16

EPS = 1e-6
NEG_INF = -1e30
ROPE_THETA = 10000.0
LOG2E = 1.4426950408889634

CHUNK = 64
ATTN_HEAD_DIM = 64
ATTN_Q_HEADS = 8
ATTN_KV_HEADS = 2
ATTN_GROUP = ATTN_Q_HEADS // ATTN_KV_HEADS
ATTN_WIDTH = ATTN_Q_HEADS * ATTN_HEAD_DIM
KV_WIDTH = ATTN_KV_HEADS * ATTN_HEAD_DIM
HGRN_HEADS = 4
HGRN_DIM = 128
HGRN_WIDTH = HGRN_HEADS * HGRN_DIM
N_MOD = 9
FFN_RES_WEIGHT = 0.5

OFF_AQ = 0
OFF_AKV = OFF_AQ + ATTN_WIDTH
OFF_HQ = OFF_AKV + 2 * KV_WIDTH
OFF_HF = OFF_HQ + HGRN_WIDTH
OFF_HI = OFF_HF + HGRN_WIDTH
OFF_HG = OFF_HI + HGRN_WIDTH

LANES = 128
SUBLANES = 8
GROUP_ROWS = 2 * CHUNK
BAND_CHUNKS = 4
BAND_ROWS = BAND_CHUNKS * CHUNK
FFN_COL_TILE = 256
ROPE_FREQS = ATTN_HEAD_DIM // 2
HGRN_SPAN = 256
VMEM_LIMIT_BYTES = 56 * 1024 * 1024

FLAG_LANE = ATTN_HEAD_DIM + BAND_CHUNKS

DIRECT_SPAN = 4


def _sigmoid(v):
    return 1.0 / (1.0 + jnp.exp(-v))


def _silu(v):
    hv = 0.5 * v
    return hv + hv * jnp.tanh(hv)


def _rms(v):
    return v * lax.rsqrt(jnp.mean(v * v, axis=-1, keepdims=True) + EPS)


def _dot(a, b):
    return jnp.dot(a, b, preferred_element_type=F32)


def _dot_nt(a, b):
    return lax.dot_general(a, b, (((1,), (1,)), ((), ())), preferred_element_type=F32)


def _dot_tn(a, b):
    return lax.dot_general(a, b, (((0,), (0,)), ((), ())), preferred_element_type=F32)


def _mod_kernel(c_ref, w_ref, b_ref, o_ref):
    c = c_ref[...]
    ca = (c * _sigmoid(c)).astype(BF16)
    o_ref[...] = _dot(ca, w_ref[...].astype(BF16)) + b_ref[...]


def _mod_call(c, w, b):
    bsz, d = c.shape
    n = w.shape[1]
    tn = 1024
    return pl.pallas_call(
        _mod_kernel,
        grid=(n // tn,),
        in_specs=[
            pl.BlockSpec((bsz, d), lambda j: (0, 0)),
            pl.BlockSpec((d, tn), lambda j: (0, j)),
            pl.BlockSpec((1, tn), lambda j: (0, j)),
        ],
        out_specs=pl.BlockSpec((bsz, tn), lambda j: (0, j)),
        out_shape=jax.ShapeDtypeStruct((bsz, n), F32),
        compiler_params=pltpu.CompilerParams(dimension_semantics=("arbitrary",)),
        name="mod",
    )(c, w, b.reshape(1, n))


def _rope_kernel(pos_ref, inv_ref, cos_ref, sin_ref):
    tr = pos_ref.shape[0]
    ang = pos_ref[...].astype(F32) * inv_ref[...]
    group = lax.broadcasted_iota(jnp.int32, (tr, LANES), 1) // ROPE_FREQS
    sign = jnp.where(group % 2 == 0, -1.0, 1.0)
    n_grp = LANES // ROPE_FREQS
    for tab, out_ref, signed in ((jnp.cos(ang), cos_ref, False), (jnp.sin(ang), sin_ref, True)):
        rolled = [tab] + [pltpu.roll(tab, ROPE_FREQS * k, 1) for k in range(1, n_grp)]
        for a in range(n_grp):
            piece = rolled[(0 - a) % n_grp]
            for kk in range(1, n_grp):
                piece = jnp.where(group == kk, rolled[(kk - a) % n_grp], piece)
            if signed:
                piece = piece * sign
            out_ref[pl.ds(a, tr, stride=n_grp), :] = piece


def _rope_call(pos_rep, inv_row):
    rows = pos_rep.shape[0]
    n_grp = LANES // ROPE_FREQS
    tr = min(rows, 512)
    out_spec = pl.BlockSpec((n_grp * tr, LANES), lambda i: (i, 0))
    return pl.pallas_call(
        _rope_kernel,
        grid=(rows // tr,),
        in_specs=[pl.BlockSpec((tr, LANES), lambda i: (i, 0)),
                  pl.BlockSpec((1, LANES), lambda i: (0, 0))],
        out_specs=[out_spec, out_spec],
        out_shape=[jax.ShapeDtypeStruct((n_grp * rows, LANES), F32)] * 2,
        compiler_params=pltpu.CompilerParams(dimension_semantics=("arbitrary",)),
        name="rope",
    )(pos_rep, inv_row)


def _resident(shape, index=None):
    index = (0,) * len(shape) if index is None else index
    return pl.BlockSpec(shape, lambda b, s: index, pipeline_mode=pl.Buffered(1))


def _ffn_kernel(x_ref, mod_ref, npre_ref, npost_ref, win_ref, wout_ref, o_ref, *, sub, d_ff):
    sh = mod_ref[0, 3 * sub:3 * sub + 1, :]
    sc = mod_ref[0, 3 * sub + 1:3 * sub + 2, :]
    gt = mod_ref[0, 3 * sub + 2:3 * sub + 3, :]
    x = x_ref[0]
    h = (_rms(x) * (npre_ref[...] * (1.0 + sc)) + sh).astype(BF16)
    acc = jnp.zeros(x.shape, F32)
    for j in range(d_ff // FFN_COL_TILE):
        lo = j * FFN_COL_TILE
        g = _dot(h, win_ref[0, :, lo:lo + FFN_COL_TILE])
        u = _dot(h, win_ref[0, :, d_ff + lo:d_ff + lo + FFN_COL_TILE])
        a = (_silu(g) * u).astype(BF16)
        acc = acc + _dot(a, wout_ref[0, lo:lo + FFN_COL_TILE, :])
    o_ref[0] = x + _rms(acc) * ((FFN_RES_WEIGHT * gt) * npost_ref[...])


def _ffn_call(x, mod, npre, npost, w_in, w_out, *, which, tm):
    bsz, seq, d = x.shape
    d_ff = w_out.shape[1]
    sub = 2 * which
    xspec = pl.BlockSpec((1, tm, d), lambda b, s: (b, s, 0))
    return pl.pallas_call(
        functools.partial(_ffn_kernel, sub=sub, d_ff=d_ff),
        grid=(bsz, seq // tm),
        in_specs=[
            xspec,
            pl.BlockSpec((1, N_MOD, d), lambda b, s: (b, 0, 0)),
            _resident((1, d)),
            _resident((1, d)),
            _resident((1, d, 2 * d_ff), (which, 0, 0)),
            _resident((1, d_ff, d), (which, 0, 0)),
        ],
        out_specs=xspec,
        out_shape=jax.ShapeDtypeStruct(x.shape, F32),
        compiler_params=pltpu.CompilerParams(
            dimension_semantics=("arbitrary", "arbitrary"),
            vmem_limit_bytes=VMEM_LIMIT_BYTES),
        name="ffn%d" % sub,
    )(x, mod, npre, npost, w_in, w_out)


def _mixer_kernel(sinks_ref, x_ref, mod_ref, npre_ref, npost_ref, win_ref, wout_ref,
                  cos_ref, sin_ref, lbl_ref, gn_ref, tril_ref, o_ref,
                  kbuf, vbuf, st_ref, mix_scr, b_scr, *, tm, layer):
    si = pl.program_id(1)
    n_groups = tm // GROUP_ROWS
    n_spans = tm // HGRN_SPAN

    lane = lax.broadcasted_iota(jnp.int32, (tm, LANES), 1)
    row = lax.broadcasted_iota(jnp.int32, (tm, LANES), 0)
    low_head = lane < ATTN_HEAD_DIM

    @pl.when(si == 0)
    def _():
        carry_lane = lax.broadcasted_iota(jnp.int32, (GROUP_ROWS, LANES), 1)
        k0 = jnp.where(carry_lane == FLAG_LANE, 1.0, 0.0).astype(BF16)
        v0 = jnp.where(carry_lane < ATTN_HEAD_DIM, 0.0, 1.0).astype(BF16)
        for hk in range(ATTN_KV_HEADS):
            kbuf[hk, 0:GROUP_ROWS, :] = k0
            vbuf[hk, 0:GROUP_ROWS, :] = v0
        st_ref[...] = jnp.zeros(st_ref.shape, F32)

    x = x_ref[0]
    sh = mod_ref[0, 3:4, :]
    sc = mod_ref[0, 4:5, :]
    gt = mod_ref[0, 5:6, :]
    h = (_rms(x) * (npre_ref[...] * (1.0 + sc)) + sh).astype(BF16)

    cos = cos_ref[0]
    sin = sin_ref[0]
    first_half = (lane % ATTN_HEAD_DIM) < (ATTN_HEAD_DIM // 2)
    chunk_id = (row // CHUNK) % BAND_CHUNKS
    key_extra = jnp.where(lane == ATTN_HEAD_DIM + chunk_id, 1.0, 0.0)
    hidden = ATTN_HEAD_DIM + (chunk_id + 1) % BAND_CHUNKS
    query_extra = jnp.where((lane == hidden) | (lane == FLAG_LANE), NEG_INF, 0.0)

    def rope(t):
        rot = jnp.where(first_half, pltpu.roll(t, LANES - 32, 1), pltpu.roll(t, 32, 1))
        return t * cos + rot * sin

    aq = _dot(h, win_ref[:, OFF_AQ:OFF_AQ + ATTN_WIDTH])
    akv = _dot(h, win_ref[:, OFF_AKV:OFF_AKV + 2 * KV_WIDTH])
    k = rope(akv[:, 0:KV_WIDTH])
    v = akv[:, KV_WIDTH:2 * KV_WIDTH]
    for src, buf, extra in ((k, kbuf, key_extra), (v, vbuf, 1.0)):
        buf[0, GROUP_ROWS:GROUP_ROWS + tm, :] = jnp.where(low_head, src, extra).astype(BF16)
        buf[1, GROUP_ROWS:GROUP_ROWS + tm, :] = jnp.where(
            low_head, pltpu.roll(src, ATTN_HEAD_DIM, 1), extra).astype(BF16)

    qh = []
    for jb in range(ATTN_Q_HEADS // 2):
        qb = rope(aq[:, jb * LANES:(jb + 1) * LANES]) * (ATTN_HEAD_DIM ** -0.5 * LOG2E)
        qh.append(jnp.where(low_head, qb, query_extra).astype(BF16))
        qh.append(jnp.where(low_head, pltpu.roll(qb, ATTN_HEAD_DIM, 1), query_extra).astype(BF16))

    low_grp = lax.broadcasted_iota(jnp.int32, (GROUP_ROWS, LANES), 1) < ATTN_HEAD_DIM

    def attn_block(g, hk):
        r0 = g * GROUP_ROWS
        heads = [hk * ATTN_GROUP + i for i in range(ATTN_GROUP)]
        qs = jnp.concatenate([qh[hd][r0:r0 + GROUP_ROWS, :] for hd in heads], axis=0)
        kband = kbuf[hk, r0:r0 + BAND_ROWS, :]
        vband = vbuf[hk, r0:r0 + BAND_ROWS, :]
        s = _dot_nt(qs, kband)
        m = jnp.max(s, axis=-1, keepdims=True)
        p = jnp.exp2(s - m)
        o = _dot(p.astype(BF16), vband)
        for pair in range(ATTN_GROUP // 2):
            res = []
            for odd in range(2):
                a0 = (2 * pair + odd) * GROUP_ROWS
                o_h = o[a0:a0 + GROUP_ROWS, :]
                sink_term = jnp.exp2(sinks_ref[heads[2 * pair + odd]] * LOG2E
                                     - m[a0:a0 + GROUP_ROWS, :])
                swapped = pltpu.roll(o_h, ATTN_HEAD_DIM, 1)
                if odd:
                    res.append(swapped / (o_h + sink_term))
                else:
                    res.append(o_h / (swapped + sink_term))
            col = (hk * (ATTN_GROUP // 2) + pair) * LANES
            mix_scr[r0:r0 + GROUP_ROWS, col:col + LANES] = (
                jnp.where(low_grp, res[0], res[1]).astype(BF16))

    lbl = lbl_ref[...]
    e_lb = jnp.exp(lbl - jnp.max(lbl, axis=0, keepdims=True))
    lb_all = jnp.sum(e_lb[0:layer + 1, :], axis=0, keepdims=True) / jnp.sum(e_lb, axis=0, keepdims=True)

    hq = _dot(h, win_ref[:, OFF_HQ:OFF_HQ + HGRN_WIDTH])
    hf = _dot(h, win_ref[:, OFF_HF:OFF_HF + HGRN_WIDTH])
    hi = _dot(h, win_ref[:, OFF_HI:OFF_HI + HGRN_WIDTH])
    hg = _dot(h, win_ref[:, OFF_HG:OFF_HG + HGRN_WIDTH])

    half_open = 0.5 * (1.0 - lb_all)
    swing = half_open * jnp.tanh(0.5 * hf)
    f = (lb_all + half_open) + swing
    one_minus_f = half_open - swing
    g = jnp.log2(f)
    g1 = g.astype(BF16)
    g2 = (g - g1.astype(F32)).astype(BF16)
    tril = tril_ref[...]
    for sp in range(n_spans):
        rows = slice(sp * HGRN_SPAN, (sp + 1) * HGRN_SPAN)
        b_scr[rows, :] = _dot(tril, g1[rows, :]) + _dot(tril, g2[rows, :])

    half_span = HGRN_SPAN // 2
    level_halves = []
    width = half_span
    while width >= DIRECT_SPAN:
        level_halves.append(width)
        width //= 2
    ti = lax.broadcasted_iota(jnp.int32, (half_span, half_span), 0)
    sj = lax.broadcasted_iota(jnp.int32, (half_span, half_span), 1)
    quad_mask = {}
    for half in level_halves[1:]:
        blk = 2 * half
        quad_mask[half] = ((ti // blk) == (sj // blk)) & ((ti % blk) >= half) & ((sj % blk) < half)
    diag = [sj == ti - delta for delta in range(DIRECT_SPAN)]
    sub_row = lax.broadcasted_iota(jnp.int32, (HGRN_SPAN, LANES), 0) % SUBLANES
    later4 = (sub_row % (2 * DIRECT_SPAN)) >= DIRECT_SPAN
    in_vreg = lax.broadcasted_iota(jnp.int32, (1, SUBLANES, LANES), 1)
    direct_ok = [(in_vreg % DIRECT_SPAN) >= delta for delta in range(DIRECT_SPAN)]

    gn = gn_ref[...]

    def hgrn_unit(hd, sp):
        cols = slice(hd * HGRN_DIM, (hd + 1) * HGRN_DIM)
        base = sp * HGRN_SPAN
        rows = slice(base, base + HGRN_SPAN)
        rq = _silu(hq[rows, cols]) * (HGRN_DIM ** -0.5)
        rk = one_minus_f[rows, cols]
        rv = hi[rows, cols].astype(BF16)
        gate = _silu(hg[rows, cols])
        b = b_scr[rows, cols]

        lvl_w = {}
        for half in level_halves:
            blk = 2 * half
            expo, operand = [], []
            for mblk in range(HGRN_SPAN // blk):
                lo = mblk * blk
                ref_row = b_scr[pl.ds(base + lo + half - 1, 1), cols]
                if half >= SUBLANES:
                    ref_half = jnp.broadcast_to(ref_row, (half, HGRN_DIM))
                    expo += [ref_half - b[lo:lo + half, :], b[lo + half:lo + blk, :] - ref_half]
                    operand += [rk[lo:lo + half, :], rq[lo + half:lo + blk, :]]
                else:
                    expo.append(jnp.broadcast_to(ref_row, (blk, HGRN_DIM)))
            if half >= SUBLANES:
                w = jnp.concatenate(operand, axis=0) * jnp.exp2(jnp.concatenate(expo, axis=0))
            else:
                dist = jnp.abs(b - jnp.concatenate(expo, axis=0))
                w = jnp.where(later4, rq, rk) * jnp.exp2(-dist)
            lvl_w[half] = w.astype(BF16)

        def grouped(t):
            return t.reshape(HGRN_SPAN // SUBLANES, SUBLANES, HGRN_DIM)

        b3, rq3, rk3 = grouped(b), grouped(rq), grouped(rk)
        direct = []
        for delta in range(DIRECT_SPAN):
            if delta == 0:
                prod = rq * rk
            else:
                gap = jnp.where(direct_ok[delta], b3 - pltpu.roll(b3, delta, 1), NEG_INF)
                prod = (rq3 * pltpu.roll(rk3, delta, 1) * jnp.exp2(gap)).reshape(HGRN_SPAN, HGRN_DIM)
            direct.append(jnp.sum(prod, axis=-1, keepdims=True))

        top = _dot_nt(lvl_w[half_span][half_span:, :], lvl_w[half_span][0:half_span, :])
        quads = []
        for qd in range(2):
            o0 = qd * half_span
            sc_q = jnp.zeros((half_span, half_span), F32)
            for delta in range(DIRECT_SPAN):
                sc_q = jnp.where(diag[delta], direct[delta][o0:o0 + half_span, :], sc_q)
            for half in level_halves[1:]:
                w_q = lvl_w[half][o0:o0 + half_span, :]
                sc_q = jnp.where(quad_mask[half], _dot_nt(w_q, w_q), sc_q)
            quads.append(sc_q.astype(BF16))
        s_hi = jnp.concatenate([top.astype(BF16), quads[1]], axis=1)

        st = st_ref[hd]
        b_last = b[HGRN_SPAN - 1:HGRN_SPAN, :]
        q_in = (rq * jnp.exp2(b)).astype(BF16)
        k_out = (rk * jnp.exp2(b_last - b)).astype(BF16)
        o = _dot_nt(q_in, st.astype(BF16)) + jnp.concatenate(
            [_dot(quads[0], rv[0:half_span, :]), _dot(s_hi, rv)], axis=0)
        st_ref[hd] = st * jnp.exp2(b_last) + _dot_tn(rv, k_out)
        o = _rms(o) * gn * gate
        mix_scr[rows, ATTN_WIDTH + hd * HGRN_DIM:ATTN_WIDTH + (hd + 1) * HGRN_DIM] = o.astype(BF16)

    attn_blocks = [(g, hk) for g in range(n_groups) for hk in range(ATTN_KV_HEADS)]
    hgrn_units = [(hd, sp) for sp in range(n_spans) for hd in range(HGRN_HEADS)]
    for i in range(max(len(attn_blocks), len(hgrn_units))):
        if i < len(attn_blocks):
            attn_block(*attn_blocks[i])
        if i < len(hgrn_units):
            hgrn_unit(*hgrn_units[i])

    for hk in range(ATTN_KV_HEADS):
        kbuf[hk, 0:GROUP_ROWS, :] = kbuf[hk, tm:tm + GROUP_ROWS, :]
        vbuf[hk, 0:GROUP_ROWS, :] = vbuf[hk, tm:tm + GROUP_ROWS, :]

    y = _dot(mix_scr[...], wout_ref[...])
    o_ref[0] = x + _rms(y) * (gt * npost_ref[...])


def _mixer_call(x, mod, npre, npost, w_in, w_out, cos_t, sin_t, sinks, lb_logits, gnorm, *, tm, layer):
    bsz, seq, d = x.shape
    d_proj = w_in.shape[1]
    d_mix = w_out.shape[0]
    assert tm % (BAND_CHUNKS * CHUNK) == 0 and tm % HGRN_SPAN == 0
    r = jnp.arange(HGRN_SPAN)
    tril = (r[None, :] <= r[:, None]).astype(BF16)
    xspec = pl.BlockSpec((1, tm, d), lambda b, s: (b, s, 0))
    tspec = pl.BlockSpec((1, tm, LANES), lambda b, s: (b, s, 0))
    return pl.pallas_call(
        functools.partial(_mixer_kernel, tm=tm, layer=layer),
        grid=(bsz, seq // tm),
        in_specs=[
            pl.BlockSpec(memory_space=pltpu.SMEM),
            xspec,
            pl.BlockSpec((1, N_MOD, d), lambda b, s: (b, 0, 0)),
            _resident((1, d)),
            _resident((1, d)),
            _resident((d, d_proj)),
            _resident((d_mix, d)),
            tspec,
            tspec,
            _resident(lb_logits.shape),
            _resident((1, HGRN_DIM)),
            _resident((HGRN_SPAN, HGRN_SPAN)),
        ],
        out_specs=xspec,
        out_shape=jax.ShapeDtypeStruct(x.shape, F32),
        scratch_shapes=[
            pltpu.VMEM((ATTN_KV_HEADS, GROUP_ROWS + tm, LANES), BF16),
            pltpu.VMEM((ATTN_KV_HEADS, GROUP_ROWS + tm, LANES), BF16),
            pltpu.VMEM((HGRN_HEADS, HGRN_DIM, HGRN_DIM), F32),
            pltpu.VMEM((tm, d_mix), BF16),
            pltpu.VMEM((tm, HGRN_WIDTH), F32),
        ],
        compiler_params=pltpu.CompilerParams(
            dimension_semantics=("arbitrary", "arbitrary"),
            vmem_limit_bytes=VMEM_LIMIT_BYTES),
        name="mixer",
    )(sinks, x, mod, npre, npost, w_in, w_out, cos_t, sin_t, lb_logits, gnorm, tril)


def kernel(x, c, positions, w_cond, b_cond, norm_pre, norm_post, ffn_w_in, ffn_w_out,
           w_mix_in, w_mix_out, attn_sinks, hgrn_lb_logits, hgrn_gnorm):
    bsz, seq, d = x.shape
    depth = w_cond.shape[0]
    tm_ffn = min(seq, 1024)
    tm_mix = min(seq, 1024)

    inv_freq = 1.0 / (ROPE_THETA ** (jnp.arange(0, ATTN_HEAD_DIM, 2, dtype=F32) / ATTN_HEAD_DIM))
    inv_row = jnp.tile(inv_freq, LANES // ROPE_FREQS)[None, :]
    pos_rep = jnp.broadcast_to(positions.reshape(-1, 1), (bsz * seq, ROPE_FREQS)).reshape(-1, LANES)
    cos_t, sin_t = _rope_call(pos_rep, inv_row)
    cos_t = cos_t.reshape(bsz, seq, LANES)
    sin_t = sin_t.reshape(bsz, seq, LANES)

    w_ffn_in = ffn_w_in.astype(BF16)
    w_ffn_out = ffn_w_out.astype(BF16)
    for layer in range(depth):
        mod = _mod_call(c, w_cond[layer], b_cond[layer]).reshape(bsz, N_MOD, d)
        npre = norm_pre[layer][:, None, :]
        npost = norm_post[layer][:, None, :]
        x = _ffn_call(x, mod, npre[0], npost[0], w_ffn_in[layer], w_ffn_out[layer],
                      which=0, tm=tm_ffn)
        x = _mixer_call(x, mod, npre[1], npost[1], w_mix_in[layer].astype(BF16),
                        w_mix_out[layer].astype(BF16), cos_t, sin_t, attn_sinks[layer],
                        hgrn_lb_logits, hgrn_gnorm[layer][None, :], tm=tm_mix, layer=layer)
        x = _ffn_call(x, mod, npre[2], npost[2], w_ffn_in[layer], w_ffn_out[layer],
                      which=1, tm=tm_ffn)
    return x
```

```python
import functools

import jax
import jax.numpy as jnp
from jax import lax
from jax.experimental import pallas as pl
from jax.experimental.pallas import tpu as pltpu

F32 = jnp.float32
BF16 = jnp.bfloat16

EPS = 1e-6
NEG_INF = -1e30
ROPE_THETA = 10000.0
LOG2E = 1.4426950408889634

CHUNK = 64
ATTN_HEAD_DIM = 64
ATTN_Q_HEADS = 8
ATTN_KV_HEADS = 2
ATTN_GROUP = ATTN_Q_HEADS // ATTN_KV_HEADS
ATTN_WIDTH = ATTN_Q_HEADS * ATTN_HEAD_DIM
KV_WIDTH = ATTN_KV_HEADS * ATTN_HEAD_DIM
HGRN_HEADS = 4
HGRN_DIM = 128
HGRN_WIDTH = HGRN_HEADS * HGRN_DIM
N_MOD = 9
FFN_RES_WEIGHT = 0.5

OFF_AQ = 0
OFF_AKV = OFF_AQ + ATTN_WIDTH
OFF_HQ = OFF_AKV + 2 * KV_WIDTH
OFF_HF = OFF_HQ + HGRN_WIDTH
OFF_HI = OFF_HF + HGRN_WIDTH
OFF_HG = OFF_HI + HGRN_WIDTH

LANES = 128
SUBLANES = 8
GROUP_ROWS = 2 * CHUNK
BAND_CHUNKS = 4
BAND_ROWS = BAND_CHUNKS * CHUNK
FFN_COL_TILE = 256
MOD_COL_TILE = 2304
ROPE_FREQS = ATTN_HEAD_DIM // 2
HGRN_SPAN = 256
VMEM_LIMIT_BYTES = 56 * 1024 * 1024

FLAG_LANE = ATTN_HEAD_DIM + BAND_CHUNKS

DIRECT_SPAN = 4


def _sigmoid(v):
    return 1.0 / (1.0 + jnp.exp(-v))


def _silu(v):
    hv = 0.5 * v
    return hv + hv * jnp.tanh(hv)


def _rms(v):
    return v * lax.rsqrt(jnp.mean(v * v, axis=-1, keepdims=True) + EPS)


def _modulated(x, gain, shift):
    return _rms(x).astype(BF16) * gain.astype(BF16) + shift.astype(BF16)


def _dot(a, b):
    return jnp.dot(a, b, preferred_element_type=F32)


def _dot_nt(a, b):
    return lax.dot_general(a, b, (((1,), (1,)), ((), ())), preferred_element_type=F32)


def _dot_tn(a, b):
    return lax.dot_general(a, b, (((0,), (0,)), ((), ())), preferred_element_type=F32)


def _mod_kernel(c_ref, w_ref, b_ref, o_ref):
    c = c_ref[...]
    ca = (c * _sigmoid(c)).astype(BF16)
    o_ref[...] = _dot(ca, w_ref[...].astype(BF16)) + b_ref[...]


def _mod_call(c, w, b):
    bsz, d = c.shape
    n = w.shape[1]
    tn = MOD_COL_TILE
    return pl.pallas_call(
        _mod_kernel,
        grid=(n // tn,),
        in_specs=[
            pl.BlockSpec((bsz, d), lambda j: (0, 0)),
            pl.BlockSpec((d, tn), lambda j: (0, j)),
            pl.BlockSpec((1, tn), lambda j: (0, j)),
        ],
        out_specs=pl.BlockSpec((bsz, tn), lambda j: (0, j)),
        out_shape=jax.ShapeDtypeStruct((bsz, n), F32),
        compiler_params=pltpu.CompilerParams(dimension_semantics=("arbitrary",)),
        name="mod",
    )(c, w, b.reshape(1, n))


def _rope_kernel(pos_ref, inv_ref, cos_ref, sin_ref):
    tr = pos_ref.shape[0]
    ang = pos_ref[...].astype(F32) * inv_ref[...]
    group = lax.broadcasted_iota(jnp.int32, (tr, LANES), 1) // ROPE_FREQS
    sign = jnp.where(group % 2 == 0, -1.0, 1.0)
    n_grp = LANES // ROPE_FREQS
    for tab, out_ref, signed in ((jnp.cos(ang), cos_ref, False), (jnp.sin(ang), sin_ref, True)):
        rolled = [tab] + [pltpu.roll(tab, ROPE_FREQS * k, 1) for k in range(1, n_grp)]
        for a in range(n_grp):
            piece = rolled[(0 - a) % n_grp]
            for kk in range(1, n_grp):
                piece = jnp.where(group == kk, rolled[(kk - a) % n_grp], piece)
            if signed:
                piece = piece * sign
            out_ref[pl.ds(a, tr, stride=n_grp), :] = piece


def _rope_call(pos_rep, inv_row):
    rows = pos_rep.shape[0]
    n_grp = LANES // ROPE_FREQS
    tr = min(rows, 512)
    out_spec = pl.BlockSpec((n_grp * tr, LANES), lambda i: (i, 0))
    return pl.pallas_call(
        _rope_kernel,
        grid=(rows // tr,),
        in_specs=[pl.BlockSpec((tr, LANES), lambda i: (i, 0)),
                  pl.BlockSpec((1, LANES), lambda i: (0, 0))],
        out_specs=[out_spec, out_spec],
        out_shape=[jax.ShapeDtypeStruct((n_grp * rows, LANES), F32)] * 2,
        compiler_params=pltpu.CompilerParams(dimension_semantics=("arbitrary",)),
        name="rope",
    )(pos_rep, inv_row)


def _resident(shape, index=None):
    index = (0,) * len(shape) if index is None else index
    return pl.BlockSpec(shape, lambda b, s: index, pipeline_mode=pl.Buffered(1))


def _ffn_kernel(x_ref, mod_ref, npre_ref, npost_ref, win_ref, wout_ref, o_ref, *, sub, d_ff):
    sh = mod_ref[0, 3 * sub:3 * sub + 1, :]
    sc = mod_ref[0, 3 * sub + 1:3 * sub + 2, :]
    gt = mod_ref[0, 3 * sub + 2:3 * sub + 3, :]
    x = x_ref[0]
    h = _modulated(x, npre_ref[...] * (1.0 + sc), sh)
    acc = jnp.zeros(x.shape, F32)
    for j in range(d_ff // FFN_COL_TILE):
        lo = j * FFN_COL_TILE
        g = _dot(h, win_ref[0, :, lo:lo + FFN_COL_TILE])
        u = _dot(h, win_ref[0, :, d_ff + lo:d_ff + lo + FFN_COL_TILE])
        a = (_silu(g) * u).astype(BF16)
        acc = acc + _dot(a, wout_ref[0, lo:lo + FFN_COL_TILE, :])
    o_ref[0] = x_ref[0] + _rms(acc) * ((FFN_RES_WEIGHT * gt) * npost_ref[...])


def _ffn_call(x, mod, npre, npost, w_in, w_out, *, which, tm):
    bsz, seq, d = x.shape
    d_ff = w_out.shape[1]
    sub = 2 * which
    xspec = pl.BlockSpec((1, tm, d), lambda b, s: (b, s, 0))
    return pl.pallas_call(
        functools.partial(_ffn_kernel, sub=sub, d_ff=d_ff),
        grid=(bsz, seq // tm),
        in_specs=[
            xspec,
            pl.BlockSpec((1, N_MOD, d), lambda b, s: (b, 0, 0)),
            _resident((1, d)),
            _resident((1, d)),
            _resident((1, d, 2 * d_ff), (which, 0, 0)),
            _resident((1, d_ff, d), (which, 0, 0)),
        ],
        out_specs=xspec,
        out_shape=jax.ShapeDtypeStruct(x.shape, F32),
        compiler_params=pltpu.CompilerParams(
            dimension_semantics=("arbitrary", "arbitrary"),
            vmem_limit_bytes=VMEM_LIMIT_BYTES),
        name="ffn%d" % sub,
    )(x, mod, npre, npost, w_in, w_out)


def _mixer_kernel(sinks_ref, x_ref, mod_ref, npre_ref, npost_ref, win_ref, wout_ref,
                  cos_ref, sin_ref, lbl_ref, gn_ref, tril_ref, o_ref,
                  kbuf, vbuf, st_ref, mix_scr, b_scr, *, tm, layer):
    si = pl.program_id(1)
    n_groups = tm // GROUP_ROWS
    n_spans = tm // HGRN_SPAN

    lane = lax.broadcasted_iota(jnp.int32, (tm, LANES), 1)
    row = lax.broadcasted_iota(jnp.int32, (tm, LANES), 0)
    low_head = lane < ATTN_HEAD_DIM

    @pl.when(si == 0)
    def _():
        carry_lane = lax.broadcasted_iota(jnp.int32, (GROUP_ROWS, LANES), 1)
        k0 = jnp.where(carry_lane == FLAG_LANE, 1.0, 0.0).astype(BF16)
        v0 = jnp.where(carry_lane < ATTN_HEAD_DIM, 0.0, 1.0).astype(BF16)
        for hk in range(ATTN_KV_HEADS):
            kbuf[hk, 0:GROUP_ROWS, :] = k0
            vbuf[hk, 0:GROUP_ROWS, :] = v0
        st_ref[...] = jnp.zeros(st_ref.shape, F32)

    x = x_ref[0]
    sh = mod_ref[0, 3:4, :]
    sc = mod_ref[0, 4:5, :]
    gt = mod_ref[0, 5:6, :]
    h = _modulated(x, npre_ref[...] * (1.0 + sc), sh)

    cos = cos_ref[0]
    sin = sin_ref[0]
    first_half = (lane % ATTN_HEAD_DIM) < (ATTN_HEAD_DIM // 2)
    chunk_id = (row // CHUNK) % BAND_CHUNKS
    key_extra = jnp.where(lane == ATTN_HEAD_DIM + chunk_id, 1.0, 0.0)
    hidden = ATTN_HEAD_DIM + (chunk_id + 1) % BAND_CHUNKS
    query_extra = jnp.where((lane == hidden) | (lane == FLAG_LANE), NEG_INF, 0.0)

    def rope(t):
        rot = jnp.where(first_half, pltpu.roll(t, LANES - 32, 1), pltpu.roll(t, 32, 1))
        return t * cos + rot * sin

    aq = _dot(h, win_ref[:, OFF_AQ:OFF_AQ + ATTN_WIDTH])
    akv = _dot(h, win_ref[:, OFF_AKV:OFF_AKV + 2 * KV_WIDTH])
    k = rope(akv[:, 0:KV_WIDTH])
    v = akv[:, KV_WIDTH:2 * KV_WIDTH]
    for src, buf, extra in ((k, kbuf, key_extra), (v, vbuf, 1.0)):
        buf[0, GROUP_ROWS:GROUP_ROWS + tm, :] = jnp.where(low_head, src, extra).astype(BF16)
        buf[1, GROUP_ROWS:GROUP_ROWS + tm, :] = jnp.where(
            low_head, pltpu.roll(src, ATTN_HEAD_DIM, 1), extra).astype(BF16)

    qh = []
    for jb in range(ATTN_Q_HEADS // 2):
        qb = rope(aq[:, jb * LANES:(jb + 1) * LANES]) * (ATTN_HEAD_DIM ** -0.5 * LOG2E)
        qh.append(jnp.where(low_head, qb, query_extra).astype(BF16))
        qh.append(jnp.where(low_head, pltpu.roll(qb, ATTN_HEAD_DIM, 1), query_extra).astype(BF16))

    low_grp = lax.broadcasted_iota(jnp.int32, (GROUP_ROWS, LANES), 1) < ATTN_HEAD_DIM

    def attn_block(g, hk):
        r0 = g * GROUP_ROWS
        heads = [hk * ATTN_GROUP + i for i in range(ATTN_GROUP)]
        qs = jnp.concatenate([qh[hd][r0:r0 + GROUP_ROWS, :] for hd in heads], axis=0)
        kband = kbuf[hk, r0:r0 + BAND_ROWS, :]
        vband = vbuf[hk, r0:r0 + BAND_ROWS, :]
        s = _dot_nt(qs, kband)
        m = jnp.max(s, axis=-1, keepdims=True)
        p = jnp.exp2(s - m)
        o = _dot(p.astype(BF16), vband)
        for pair in range(ATTN_GROUP // 2):
            res = []
            for odd in range(2):
                a0 = (2 * pair + odd) * GROUP_ROWS
                o_h = o[a0:a0 + GROUP_ROWS, :]
                sink_term = jnp.exp2(sinks_ref[heads[2 * pair + odd]] * LOG2E
                                     - m[a0:a0 + GROUP_ROWS, :])
                swapped = pltpu.roll(o_h, ATTN_HEAD_DIM, 1)
                if odd:
                    res.append(swapped / (o_h + sink_term))
                else:
                    res.append(o_h / (swapped + sink_term))
            col = (hk * (ATTN_GROUP // 2) + pair) * LANES
            mix_scr[r0:r0 + GROUP_ROWS, col:col + LANES] = (
                jnp.where(low_grp, res[0], res[1]).astype(BF16))

    lbl = lbl_ref[...]
    e_lb = jnp.exp(lbl - jnp.max(lbl, axis=0, keepdims=True))
    lb_all = jnp.sum(e_lb[0:layer + 1, :], axis=0, keepdims=True) / jnp.sum(e_lb, axis=0, keepdims=True)

    hq = _dot(h, win_ref[:, OFF_HQ:OFF_HQ + HGRN_WIDTH])
    hf = _dot(h, win_ref[:, OFF_HF:OFF_HF + HGRN_WIDTH])
    hi = _dot(h, win_ref[:, OFF_HI:OFF_HI + HGRN_WIDTH])
    hg = _dot(h, win_ref[:, OFF_HG:OFF_HG + HGRN_WIDTH])

    half_open = 0.5 * (1.0 - lb_all)
    swing = half_open * jnp.tanh(0.5 * hf)
    f = (lb_all + half_open) + swing
    one_minus_f = half_open - swing
    g = jnp.log2(f)
    g1 = g.astype(BF16)
    g2 = (g - g1.astype(F32)).astype(BF16)
    tril = tril_ref[...]
    for sp in range(n_spans):
        rows = slice(sp * HGRN_SPAN, (sp + 1) * HGRN_SPAN)
        b_scr[rows, :] = _dot(tril, g1[rows, :]) + _dot(tril, g2[rows, :])

    half_span = HGRN_SPAN // 2
    level_halves = []
    width = half_span
    while width >= DIRECT_SPAN:
        level_halves.append(width)
        width //= 2
    ti = lax.broadcasted_iota(jnp.int32, (half_span, half_span), 0)
    sj = lax.broadcasted_iota(jnp.int32, (half_span, half_span), 1)
    quad_mask = {}
    for half in level_halves[1:]:
        blk = 2 * half
        quad_mask[half] = ((ti // blk) == (sj // blk)) & ((ti % blk) >= half) & ((sj % blk) < half)
    diag = [sj == ti - delta for delta in range(DIRECT_SPAN)]
    sub_row = lax.broadcasted_iota(jnp.int32, (HGRN_SPAN, LANES), 0) % SUBLANES
    later4 = (sub_row % (2 * DIRECT_SPAN)) >= DIRECT_SPAN
    in_vreg = lax.broadcasted_iota(jnp.int32, (1, SUBLANES, LANES), 1)
    direct_ok = [(in_vreg % DIRECT_SPAN) >= delta for delta in range(DIRECT_SPAN)]

    gn = gn_ref[...]

    def hgrn_unit(hd, sp):
        cols = slice(hd * HGRN_DIM, (hd + 1) * HGRN_DIM)
        base = sp * HGRN_SPAN
        rows = slice(base, base + HGRN_SPAN)
        rq = _silu(hq[rows, cols]) * (HGRN_DIM ** -0.5)
        rk = one_minus_f[rows, cols]
        rv = hi[rows, cols].astype(BF16)
        gate = _silu(hg[rows, cols])
        b = b_scr[rows, cols]

        lvl_w = {}
        for half in level_halves:
            blk = 2 * half
            expo, operand = [], []
            for mblk in range(HGRN_SPAN // blk):
                lo = mblk * blk
                ref_row = b_scr[pl.ds(base + lo + half - 1, 1), cols]
                if half >= SUBLANES:
                    ref_half = jnp.broadcast_to(ref_row, (half, HGRN_DIM))
                    expo += [ref_half - b[lo:lo + half, :], b[lo + half:lo + blk, :] - ref_half]
                    operand += [rk[lo:lo + half, :], rq[lo + half:lo + blk, :]]
                else:
                    expo.append(jnp.broadcast_to(ref_row, (blk, HGRN_DIM)))
            if half >= SUBLANES:
                w = jnp.concatenate(operand, axis=0) * jnp.exp2(jnp.concatenate(expo, axis=0))
            else:
                dist = jnp.abs(b - jnp.concatenate(expo, axis=0))
                w = jnp.where(later4, rq, rk) * jnp.exp2(-dist)
            lvl_w[half] = w.astype(BF16)

        def grouped(t):
            return t.reshape(HGRN_SPAN // SUBLANES, SUBLANES, HGRN_DIM)

        b3, rq3, rk3 = grouped(b), grouped(rq), grouped(rk)
        direct = []
        for delta in range(DIRECT_SPAN):
            if delta == 0:
                prod = rq * rk
            else:
                gap = jnp.where(direct_ok[delta], b3 - pltpu.roll(b3, delta, 1), NEG_INF)
                prod = (rq3 * pltpu.roll(rk3, delta, 1) * jnp.exp2(gap)).reshape(HGRN_SPAN, HGRN_DIM)
            direct.append(jnp.sum(prod, axis=-1, keepdims=True))

        top = _dot_nt(lvl_w[half_span][half_span:, :], lvl_w[half_span][0:half_span, :])
        quads = []
        for qd in range(2):
            o0 = qd * half_span
            sc_q = jnp.zeros((half_span, half_span), F32)
            for delta in range(DIRECT_SPAN):
                sc_q = jnp.where(diag[delta], direct[delta][o0:o0 + half_span, :], sc_q)
            for half in level_halves[1:]:
                w_q = lvl_w[half][o0:o0 + half_span, :]
                sc_q = jnp.where(quad_mask[half], _dot_nt(w_q, w_q), sc_q)
            quads.append(sc_q.astype(BF16))
        s_hi = jnp.concatenate([top.astype(BF16), quads[1]], axis=1)

        st = st_ref[hd]
        b_last = b[HGRN_SPAN - 1:HGRN_SPAN, :]
        q_in = (rq * jnp.exp2(b)).astype(BF16)
        k_out = (rk * jnp.exp2(b_last - b)).astype(BF16)
        o = _dot_nt(q_in, st.astype(BF16)) + jnp.concatenate(
            [_dot(quads[0], rv[0:half_span, :]), _dot(s_hi, rv)], axis=0)
        st_ref[hd] = st * jnp.exp2(b_last) + _dot_tn(rv, k_out)
        o = _rms(o) * gn * gate
        mix_scr[rows, ATTN_WIDTH + hd * HGRN_DIM:ATTN_WIDTH + (hd + 1) * HGRN_DIM] = o.astype(BF16)

    attn_blocks = [(g, hk) for g in range(n_groups) for hk in range(ATTN_KV_HEADS)]
    hgrn_units = [(hd, sp) for sp in range(n_spans) for hd in range(HGRN_HEADS)]
    for i in range(max(len(attn_blocks), len(hgrn_units))):
        if i < len(attn_blocks):
            attn_block(*attn_blocks[i])
        if i < len(hgrn_units):
            hgrn_unit(*hgrn_units[i])

    for hk in range(ATTN_KV_HEADS):
        kbuf[hk, 0:GROUP_ROWS, :] = kbuf[hk, tm:tm + GROUP_ROWS, :]
        vbuf[hk, 0:GROUP_ROWS, :] = vbuf[hk, tm:tm + GROUP_ROWS, :]

    y = _dot(mix_scr[...], wout_ref[...])
    o_ref[0] = x_ref[0] + _rms(y) * (gt * npost_ref[...])


def _mixer_call(x, mod, npre, npost, w_in, w_out, cos_t, sin_t, sinks, lb_logits, gnorm, *, tm, layer):
    bsz, seq, d = x.shape
    d_proj = w_in.shape[1]
    d_mix = w_out.shape[0]
    assert tm % (BAND_CHUNKS * CHUNK) == 0 and tm % HGRN_SPAN == 0
    r = jnp.arange(HGRN_SPAN)
    tril = (r[None, :] <= r[:, None]).astype(BF16)
    xspec = pl.BlockSpec((1, tm, d), lambda b, s: (b, s, 0))
    tspec = pl.BlockSpec((1, tm, LANES), lambda b, s: (b, s, 0))
    return pl.pallas_call(
        functools.partial(_mixer_kernel, tm=tm, layer=layer),
        grid=(bsz, seq // tm),
        in_specs=[
            pl.BlockSpec(memory_space=pltpu.SMEM),
            xspec,
            pl.BlockSpec((1, N_MOD, d), lambda b, s: (b, 0, 0)),
            _resident((1, d)),
            _resident((1, d)),
            _resident((d, d_proj)),
            _resident((d_mix, d)),
            tspec,
            tspec,
            _resident(lb_logits.shape),
            _resident((1, HGRN_DIM)),
            _resident((HGRN_SPAN, HGRN_SPAN)),
        ],
        out_specs=xspec,
        out_shape=jax.ShapeDtypeStruct(x.shape, F32),
        scratch_shapes=[
            pltpu.VMEM((ATTN_KV_HEADS, GROUP_ROWS + tm, LANES), BF16),
            pltpu.VMEM((ATTN_KV_HEADS, GROUP_ROWS + tm, LANES), BF16),
            pltpu.VMEM((HGRN_HEADS, HGRN_DIM, HGRN_DIM), F32),
            pltpu.VMEM((tm, d_mix), BF16),
            pltpu.VMEM((tm, HGRN_WIDTH), F32),
        ],
        compiler_params=pltpu.CompilerParams(
            dimension_semantics=("arbitrary", "arbitrary"),
            vmem_limit_bytes=VMEM_LIMIT_BYTES),
        name="mixer",
    )(sinks, x, mod, npre, npost, w_in, w_out, cos_t, sin_t, lb_logits, gnorm, tril)


def kernel(x, c, positions, w_cond, b_cond, norm_pre, norm_post, ffn_w_in, ffn_w_out,
           w_mix_in, w_mix_out, attn_sinks, hgrn_lb_logits, hgrn_gnorm):
    bsz, seq, d = x.shape
    depth = w_cond.shape[0]
    tm_ffn = min(seq, 1024)
    tm_mix = min(seq, 512)

    inv_freq = 1.0 / (ROPE_THETA ** (jnp.arange(0, ATTN_HEAD_DIM, 2, dtype=F32) / ATTN_HEAD_DIM))
    inv_row = jnp.tile(inv_freq, LANES // ROPE_FREQS)[None, :]
    pos_rep = jnp.broadcast_to(positions.reshape(-1, 1), (bsz * seq, ROPE_FREQS)).reshape(-1, LANES)
    cos_t, sin_t = _rope_call(pos_rep, inv_row)
    cos_t = cos_t.reshape(bsz, seq, LANES)
    sin_t = sin_t.reshape(bsz, seq, LANES)

    w_ffn_in = ffn_w_in.astype(BF16)
    w_ffn_out = ffn_w_out.astype(BF16)
    for layer in range(depth):
        mod = _mod_call(c, w_cond[layer], b_cond[layer]).reshape(bsz, N_MOD, d)
        npre = norm_pre[layer][:, None, :]
        npost = norm_post[layer][:, None, :]
        x = _ffn_call(x, mod, npre[0], npost[0], w_ffn_in[layer], w_ffn_out[layer],
                      which=0, tm=tm_ffn)
        x = _mixer_call(x, mod, npre[1], npost[1], w_mix_in[layer].astype(BF16),
                        w_mix_out[layer].astype(BF16), cos_t, sin_t, attn_sinks[layer],
                        hgrn_lb_logits, hgrn_gnorm[layer][None, :], tm=tm_mix, layer=layer)
        x = _ffn_call(x, mod, npre[2], npost[2], w_ffn_in[layer], w_ffn_out[layer],
                      which=1, tm=tm_ffn)
    return x
```

```python
import functools

import jax
import jax.numpy as jnp
from jax import lax
from jax.experimental import pallas as pl
from jax.experimental.pallas import tpu as pltpu

F32 = jnp.float32
BF16 = jnp.bfloat16

EPS = 1e-6
NEG_INF = -1e30
ROPE_THETA = 10000.0
LOG2E = 1.4426950408889634

CHUNK = 64
ATTN_HEAD_DIM = 64
ATTN_Q_HEADS = 8
ATTN_KV_HEADS = 2
ATTN_GROUP = ATTN_Q_HEADS // ATTN_KV_HEADS
ATTN_WIDTH = ATTN_Q_HEADS * ATTN_HEAD_DIM
KV_WIDTH = ATTN_KV_HEADS * ATTN_HEAD_DIM
HGRN_HEADS = 4
HGRN_DIM = 128
HGRN_WIDTH = HGRN_HEADS * HGRN_DIM
N_MOD = 9
FFN_RES_WEIGHT = 0.5

OFF_AQ = 0
OFF_AKV = OFF_AQ + ATTN_WIDTH
OFF_HQ = OFF_AKV + 2 * KV_WIDTH
OFF_HF = OFF_HQ + HGRN_WIDTH
OFF_HI = OFF_HF + HGRN_WIDTH
OFF_HG = OFF_HI + HGRN_WIDTH

LANES = 128
SUBLANES = 8
GROUP_ROWS = 2 * CHUNK
BAND_CHUNKS = 4
BAND_ROWS = BAND_CHUNKS * CHUNK
FFN_COL_TILE = 256
MOD_COL_TILE = 2304
ROPE_FREQS = ATTN_HEAD_DIM // 2
HGRN_SPAN = 256
VMEM_LIMIT_BYTES = 56 * 1024 * 1024

FLAG_LANE = ATTN_HEAD_DIM + BAND_CHUNKS

DIRECT_SPAN = 4


def _sigmoid(v):
    return 1.0 / (1.0 + jnp.exp(-v))


def _silu(v):
    hv = 0.5 * v
    return hv + hv * jnp.tanh(hv)


def _rms(v):
    return v * lax.rsqrt(jnp.mean(v * v, axis=-1, keepdims=True) + EPS)


def _modulated(x, gain, shift):
    return _rms(x).astype(BF16) * gain.astype(BF16) + shift.astype(BF16)


def _dot(a, b):
    return jnp.dot(a, b, preferred_element_type=F32)


def _dot_nt(a, b):
    return lax.dot_general(a, b, (((1,), (1,)), ((), ())), preferred_element_type=F32)


def _dot_tn(a, b):
    return lax.dot_general(a, b, (((0,), (0,)), ((), ())), preferred_element_type=F32)


def _mod_kernel(c_ref, w_ref, b_ref, o_ref):
    c = c_ref[...]
    ca = (c * _sigmoid(c)).astype(BF16)
    o_ref[...] = _dot(ca, w_ref[...].astype(BF16)) + b_ref[...]


def _mod_call(c, w, b):
    bsz, d = c.shape
    n = w.shape[1]
    tn = MOD_COL_TILE
    return pl.pallas_call(
        _mod_kernel,
        grid=(n // tn,),
        in_specs=[
            pl.BlockSpec((bsz, d), lambda j: (0, 0)),
            pl.BlockSpec((d, tn), lambda j: (0, j)),
            pl.BlockSpec((1, tn), lambda j: (0, j)),
        ],
        out_specs=pl.BlockSpec((bsz, tn), lambda j: (0, j)),
        out_shape=jax.ShapeDtypeStruct((bsz, n), F32),
        compiler_params=pltpu.CompilerParams(dimension_semantics=("arbitrary",)),
        name="mod",
    )(c, w, b.reshape(1, n))


def _rope_kernel(pos_ref, inv_ref, cos_ref, sin_ref):
    tr = pos_ref.shape[0]
    ang = pos_ref[...].astype(F32) * inv_ref[...]
    group = lax.broadcasted_iota(jnp.int32, (tr, LANES), 1) // ROPE_FREQS
    sign = jnp.where(group % 2 == 0, -1.0, 1.0)
    n_grp = LANES // ROPE_FREQS
    for tab, out_ref, signed in ((jnp.cos(ang), cos_ref, False), (jnp.sin(ang), sin_ref, True)):
        rolled = [tab] + [pltpu.roll(tab, ROPE_FREQS * k, 1) for k in range(1, n_grp)]
        for a in range(n_grp):
            piece = rolled[(0 - a) % n_grp]
            for kk in range(1, n_grp):
                piece = jnp.where(group == kk, rolled[(kk - a) % n_grp], piece)
            if signed:
                piece = piece * sign
            out_ref[pl.ds(a, tr, stride=n_grp), :] = piece


def _rope_call(pos_rep, inv_row):
    rows = pos_rep.shape[0]
    n_grp = LANES // ROPE_FREQS
    tr = min(rows, 512)
    out_spec = pl.BlockSpec((n_grp * tr, LANES), lambda i: (i, 0))
    return pl.pallas_call(
        _rope_kernel,
        grid=(rows // tr,),
        in_specs=[pl.BlockSpec((tr, LANES), lambda i: (i, 0)),
                  pl.BlockSpec((1, LANES), lambda i: (0, 0))],
        out_specs=[out_spec, out_spec],
        out_shape=[jax.ShapeDtypeStruct((n_grp * rows, LANES), F32)] * 2,
        compiler_params=pltpu.CompilerParams(dimension_semantics=("arbitrary",)),
        name="rope",
    )(pos_rep, inv_row)


def _resident(shape, index=None):
    index = (0,) * len(shape) if index is None else index
    return pl.BlockSpec(shape, lambda b, s: index, pipeline_mode=pl.Buffered(1))


def _ffn_kernel(x_ref, mod_ref, npre_ref, npost_ref, win_ref, wout_ref, o_ref, *, sub, d_ff):
    sh = mod_ref[0, 3 * sub:3 * sub + 1, :]
    sc = mod_ref[0, 3 * sub + 1:3 * sub + 2, :]
    gt = mod_ref[0, 3 * sub + 2:3 * sub + 3, :]
    x = x_ref[0]
    h = _modulated(x, npre_ref[...] * (1.0 + sc), sh)
    acc = jnp.zeros(x.shape, F32)
    for j in range(d_ff // FFN_COL_TILE):
        lo = j * FFN_COL_TILE
        g = _dot(h, win_ref[0, :, lo:lo + FFN_COL_TILE])
        u = _dot(h, win_ref[0, :, d_ff + lo:d_ff + lo + FFN_COL_TILE])
        a = (_silu(g) * u).astype(BF16)
        acc = acc + _dot(a, wout_ref[0, lo:lo + FFN_COL_TILE, :])
    o_ref[0] = x_ref[0] + _rms(acc) * ((FFN_RES_WEIGHT * gt) * npost_ref[...])


def _ffn_call(x, mod, npre, npost, w_in, w_out, *, which, tm):
    bsz, seq, d = x.shape
    d_ff = w_out.shape[1]
    sub = 2 * which
    xspec = pl.BlockSpec((1, tm, d), lambda b, s: (b, s, 0))
    return pl.pallas_call(
        functools.partial(_ffn_kernel, sub=sub, d_ff=d_ff),
        grid=(bsz, seq // tm),
        in_specs=[
            xspec,
            pl.BlockSpec((1, N_MOD, d), lambda b, s: (b, 0, 0)),
            _resident((1, d)),
            _resident((1, d)),
            _resident((1, d, 2 * d_ff), (which, 0, 0)),
            _resident((1, d_ff, d), (which, 0, 0)),
        ],
        out_specs=xspec,
        out_shape=jax.ShapeDtypeStruct(x.shape, F32),
        compiler_params=pltpu.CompilerParams(
            dimension_semantics=("arbitrary", "arbitrary"),
            vmem_limit_bytes=VMEM_LIMIT_BYTES),
        name="ffn%d" % sub,
    )(x, mod, npre, npost, w_in, w_out)


def _mixer_kernel(sinks_ref, x_ref, mod_ref, npre_ref, npost_ref, win_ref, wout_ref,
                  cos_ref, sin_ref, lbl_ref, gn_ref, tril_ref, o_ref,
                  kbuf, vbuf, st_ref, mix_scr, b_scr, *, tm, layer):
    si = pl.program_id(1)
    n_groups = tm // GROUP_ROWS
    n_spans = tm // HGRN_SPAN

    lane = lax.broadcasted_iota(jnp.int32, (tm, LANES), 1)
    row = lax.broadcasted_iota(jnp.int32, (tm, LANES), 0)
    low_head = lane < ATTN_HEAD_DIM

    @pl.when(si == 0)
    def _():
        carry_lane = lax.broadcasted_iota(jnp.int32, (GROUP_ROWS, LANES), 1)
        k0 = jnp.where(carry_lane == FLAG_LANE, 1.0, 0.0).astype(BF16)
        v0 = jnp.where(carry_lane < ATTN_HEAD_DIM, 0.0, 1.0).astype(BF16)
        for hk in range(ATTN_KV_HEADS):
            kbuf[hk, 0:GROUP_ROWS, :] = k0
            vbuf[hk, 0:GROUP_ROWS, :] = v0
        st_ref[...] = jnp.zeros(st_ref.shape, F32)

    x = x_ref[0]
    sh = mod_ref[0, 3:4, :]
    sc = mod_ref[0, 4:5, :]
    gt = mod_ref[0, 5:6, :]
    h = _modulated(x, npre_ref[...] * (1.0 + sc), sh)

    cos = cos_ref[0]
    sin = sin_ref[0]
    first_half = (lane % ATTN_HEAD_DIM) < (ATTN_HEAD_DIM // 2)
    chunk_id = (row // CHUNK) % BAND_CHUNKS
    key_extra = jnp.where(lane == ATTN_HEAD_DIM + chunk_id, 1.0, 0.0)
    hidden = ATTN_HEAD_DIM + (chunk_id + 1) % BAND_CHUNKS
    query_extra = jnp.where((lane == hidden) | (lane == FLAG_LANE), NEG_INF, 0.0)

    def rope(t):
        rot = jnp.where(first_half, pltpu.roll(t, LANES - 32, 1), pltpu.roll(t, 32, 1))
        return t * cos + rot * sin

    aq = _dot(h, win_ref[:, OFF_AQ:OFF_AQ + ATTN_WIDTH])
    akv = _dot(h, win_ref[:, OFF_AKV:OFF_AKV + 2 * KV_WIDTH])
    k = rope(akv[:, 0:KV_WIDTH])
    v = akv[:, KV_WIDTH:2 * KV_WIDTH]
    for src, buf, extra in ((k, kbuf, key_extra), (v, vbuf, 1.0)):
        buf[0, GROUP_ROWS:GROUP_ROWS + tm, :] = jnp.where(low_head, src, extra).astype(BF16)
        buf[1, GROUP_ROWS:GROUP_ROWS + tm, :] = jnp.where(
            low_head, pltpu.roll(src, ATTN_HEAD_DIM, 1), extra).astype(BF16)

    qh = []
    for jb in range(ATTN_Q_HEADS // 2):
        qb = rope(aq[:, jb * LANES:(jb + 1) * LANES]) * (ATTN_HEAD_DIM ** -0.5 * LOG2E)
        qh.append(jnp.where(low_head, qb, query_extra).astype(BF16))
        qh.append(jnp.where(low_head, pltpu.roll(qb, ATTN_HEAD_DIM, 1), query_extra).astype(BF16))

    low_grp = lax.broadcasted_iota(jnp.int32, (GROUP_ROWS, LANES), 1) < ATTN_HEAD_DIM

    def attn_block(g, hk):
        r0 = g * GROUP_ROWS
        heads = [hk * ATTN_GROUP + i for i in range(ATTN_GROUP)]
        qs = jnp.concatenate([qh[hd][r0:r0 + GROUP_ROWS, :] for hd in heads], axis=0)
        kband = kbuf[hk, r0:r0 + BAND_ROWS, :]
        vband = vbuf[hk, r0:r0 + BAND_ROWS, :]
        s = _dot_nt(qs, kband)
        m = jnp.max(s, axis=-1, keepdims=True)
        p = jnp.exp2(s - m)
        o = _dot(p.astype(BF16), vband)
        for pair in range(ATTN_GROUP // 2):
            res = []
            for odd in range(2):
                a0 = (2 * pair + odd) * GROUP_ROWS
                o_h = o[a0:a0 + GROUP_ROWS, :]
                sink_term = jnp.exp2(sinks_ref[heads[2 * pair + odd]] * LOG2E
                                     - m[a0:a0 + GROUP_ROWS, :])
                swapped = pltpu.roll(o_h, ATTN_HEAD_DIM, 1)
                if odd:
                    res.append(swapped / (o_h + sink_term))
                else:
                    res.append(o_h / (swapped + sink_term))
            col = (hk * (ATTN_GROUP // 2) + pair) * LANES
            mix_scr[r0:r0 + GROUP_ROWS, col:col + LANES] = (
                jnp.where(low_grp, res[0], res[1]).astype(BF16))

    lbl = lbl_ref[...]
    e_lb = jnp.exp(lbl - jnp.max(lbl, axis=0, keepdims=True))
    lb_all = jnp.sum(e_lb[0:layer + 1, :], axis=0, keepdims=True) / jnp.sum(e_lb, axis=0, keepdims=True)

    hq = _dot(h, win_ref[:, OFF_HQ:OFF_HQ + HGRN_WIDTH])
    hf = _dot(h, win_ref[:, OFF_HF:OFF_HF + HGRN_WIDTH])
    hi = _dot(h, win_ref[:, OFF_HI:OFF_HI + HGRN_WIDTH])
    hg = _dot(h, win_ref[:, OFF_HG:OFF_HG + HGRN_WIDTH])

    f = lb_all + (1.0 - lb_all) * _sigmoid(hf)
    one_minus_f = 1.0 - f
    g = jnp.log2(f)
    g1 = g.astype(BF16)
    g2 = (g - g1.astype(F32)).astype(BF16)
    tril = tril_ref[...]
    for sp in range(n_spans):
        rows = slice(sp * HGRN_SPAN, (sp + 1) * HGRN_SPAN)
        b_scr[rows, :] = _dot(tril, g1[rows, :]) + _dot(tril, g2[rows, :])

    half_span = HGRN_SPAN // 2
    level_halves = []
    width = half_span
    while width >= DIRECT_SPAN:
        level_halves.append(width)
        width //= 2
    ti = lax.broadcasted_iota(jnp.int32, (half_span, half_span), 0)
    sj = lax.broadcasted_iota(jnp.int32, (half_span, half_span), 1)
    quad_mask = {}
    for half in level_halves[1:]:
        blk = 2 * half
        quad_mask[half] = ((ti // blk) == (sj // blk)) & ((ti % blk) >= half) & ((sj % blk) < half)
    diag = [sj == ti - delta for delta in range(DIRECT_SPAN)]
    sub_row = lax.broadcasted_iota(jnp.int32, (HGRN_SPAN, LANES), 0) % SUBLANES
    later4 = (sub_row % (2 * DIRECT_SPAN)) >= DIRECT_SPAN
    in_vreg = lax.broadcasted_iota(jnp.int32, (1, SUBLANES, LANES), 1)
    direct_ok = [(in_vreg % DIRECT_SPAN) >= delta for delta in range(DIRECT_SPAN)]

    gn = gn_ref[...]

    def hgrn_unit(hd, sp):
        cols = slice(hd * HGRN_DIM, (hd + 1) * HGRN_DIM)
        base = sp * HGRN_SPAN
        rows = slice(base, base + HGRN_SPAN)
        rq = _silu(hq[rows, cols]) * (HGRN_DIM ** -0.5)
        rk = one_minus_f[rows, cols]
        rv = hi[rows, cols].astype(BF16)
        gate = _silu(hg[rows, cols])
        b = b_scr[rows, cols]

        lvl_w = {}
        for half in level_halves:
            blk = 2 * half
            expo, operand = [], []
            for mblk in range(HGRN_SPAN // blk):
                lo = mblk * blk
                ref_row = b_scr[pl.ds(base + lo + half - 1, 1), cols]
                if half >= SUBLANES:
                    ref_half = jnp.broadcast_to(ref_row, (half, HGRN_DIM))
                    expo += [ref_half - b[lo:lo + half, :], b[lo + half:lo + blk, :] - ref_half]
                    operand += [rk[lo:lo + half, :], rq[lo + half:lo + blk, :]]
                else:
                    expo.append(jnp.broadcast_to(ref_row, (blk, HGRN_DIM)))
            if half >= SUBLANES:
                w = jnp.concatenate(operand, axis=0) * jnp.exp2(jnp.concatenate(expo, axis=0))
            else:
                dist = jnp.abs(b - jnp.concatenate(expo, axis=0))
                w = jnp.where(later4, rq, rk) * jnp.exp2(-dist)
            lvl_w[half] = w.astype(BF16)

        def grouped(t):
            return t.reshape(HGRN_SPAN // SUBLANES, SUBLANES, HGRN_DIM)

        b3, rq3, rk3 = grouped(b), grouped(rq), grouped(rk)
        direct = []
        for delta in range(DIRECT_SPAN):
            if delta == 0:
                prod = rq * rk
            else:
                gap = jnp.where(direct_ok[delta], b3 - pltpu.roll(b3, delta, 1), NEG_INF)
                prod = (rq3 * pltpu.roll(rk3, delta, 1) * jnp.exp2(gap)).reshape(HGRN_SPAN, HGRN_DIM)
            direct.append(jnp.sum(prod, axis=-1, keepdims=True))

        top = _dot_nt(lvl_w[half_span][half_span:, :], lvl_w[half_span][0:half_span, :])
        quads = []
        for qd in range(2):
            o0 = qd * half_span
            sc_q = jnp.zeros((half_span, half_span), F32)
            for delta in range(DIRECT_SPAN):
                sc_q = jnp.where(diag[delta], direct[delta][o0:o0 + half_span, :], sc_q)
            for half in level_halves[1:]:
                w_q = lvl_w[half][o0:o0 + half_span, :]
                sc_q = jnp.where(quad_mask[half], _dot_nt(w_q, w_q), sc_q)
            quads.append(sc_q.astype(BF16))
        s_hi = jnp.concatenate([top.astype(BF16), quads[1]], axis=1)

        st = st_ref[hd]
        b_last = b[HGRN_SPAN - 1:HGRN_SPAN, :]
        q_in = (rq * jnp.exp2(b)).astype(BF16)
        k_out = (rk * jnp.exp2(b_last - b)).astype(BF16)
        o = _dot_nt(q_in, st.astype(BF16)) + jnp.concatenate(
            [_dot(quads[0], rv[0:half_span, :]), _dot(s_hi, rv)], axis=0)
        st_ref[hd] = st * jnp.exp2(b_last) + _dot_tn(rv, k_out)
        o = _rms(o) * gn * gate
        mix_scr[rows, ATTN_WIDTH + hd * HGRN_DIM:ATTN_WIDTH + (hd + 1) * HGRN_DIM] = o.astype(BF16)

    attn_blocks = [(g, hk) for g in range(n_groups) for hk in range(ATTN_KV_HEADS)]
    hgrn_units = [(hd, sp) for sp in range(n_spans) for hd in range(HGRN_HEADS)]
    for i in range(max(len(attn_blocks), len(hgrn_units))):
        if i < len(attn_blocks):
            attn_block(*attn_blocks[i])
        if i < len(hgrn_units):
            hgrn_unit(*hgrn_units[i])

    for hk in range(ATTN_KV_HEADS):
        kbuf[hk, 0:GROUP_ROWS, :] = kbuf[hk, tm:tm + GROUP_ROWS, :]
        vbuf[hk, 0:GROUP_ROWS, :] = vbuf[hk, tm:tm + GROUP_ROWS, :]

    y = _dot(mix_scr[...], wout_ref[...])
    o_ref[0] = x_ref[0] + _rms(y) * (gt * npost_ref[...])


def _mixer_call(x, mod, npre, npost, w_in, w_out, cos_t, sin_t, sinks, lb_logits, gnorm, *, tm, layer):
    bsz, seq, d = x.shape
    d_proj = w_in.shape[1]
    d_mix = w_out.shape[0]
    assert tm % (BAND_CHUNKS * CHUNK) == 0 and tm % HGRN_SPAN == 0
    r = jnp.arange(HGRN_SPAN)
    tril = (r[None, :] <= r[:, None]).astype(BF16)
    xspec = pl.BlockSpec((1, tm, d), lambda b, s: (b, s, 0))
    tspec = pl.BlockSpec((1, tm, LANES), lambda b, s: (b, s, 0))
    return pl.pallas_call(
        functools.partial(_mixer_kernel, tm=tm, layer=layer),
        grid=(bsz, seq // tm),
        in_specs=[
            pl.BlockSpec(memory_space=pltpu.SMEM),
            xspec,
            pl.BlockSpec((1, N_MOD, d), lambda b, s: (b, 0, 0)),
            _resident((1, d)),
            _resident((1, d)),
            _resident((d, d_proj)),
            _resident((d_mix, d)),
            tspec,
            tspec,
            _resident(lb_logits.shape),
            _resident((1, HGRN_DIM)),
            _resident((HGRN_SPAN, HGRN_SPAN)),
        ],
        out_specs=xspec,
        out_shape=jax.ShapeDtypeStruct(x.shape, F32),
        scratch_shapes=[
            pltpu.VMEM((ATTN_KV_HEADS, GROUP_ROWS + tm, LANES), BF16),
            pltpu.VMEM((ATTN_KV_HEADS, GROUP_ROWS + tm, LANES), BF16),
            pltpu.VMEM((HGRN_HEADS, HGRN_DIM, HGRN_DIM), F32),
            pltpu.VMEM((tm, d_mix), BF16),
            pltpu.VMEM((tm, HGRN_WIDTH), F32),
        ],
        compiler_params=pltpu.CompilerParams(
            dimension_semantics=("arbitrary", "arbitrary"),
            vmem_limit_bytes=VMEM_LIMIT_BYTES),
        name="mixer",
    )(sinks, x, mod, npre, npost, w_in, w_out, cos_t, sin_t, lb_logits, gnorm, tril)


def kernel(x, c, positions, w_cond, b_cond, norm_pre, norm_post, ffn_w_in, ffn_w_out,
           w_mix_in, w_mix_out, attn_sinks, hgrn_lb_logits, hgrn_gnorm):
    bsz, seq, d = x.shape
    depth = w_cond.shape[0]
    tm_ffn = min(seq, 1024)
    tm_mix = min(seq, 512)

    inv_freq = 1.0 / (ROPE_THETA ** (jnp.arange(0, ATTN_HEAD_DIM, 2, dtype=F32) / ATTN_HEAD_DIM))
    inv_row = jnp.tile(inv_freq, LANES // ROPE_FREQS)[None, :]
    pos_rep = jnp.broadcast_to(positions.reshape(-1, 1), (bsz * seq, ROPE_FREQS)).reshape(-1, LANES)
    cos_t, sin_t = _rope_call(pos_rep, inv_row)
    cos_t = cos_t.reshape(bsz, seq, LANES)
    sin_t = sin_t.reshape(bsz, seq, LANES)

    w_ffn_in = ffn_w_in.astype(BF16)
    w_ffn_out = ffn_w_out.astype(BF16)
    for layer in range(depth):
        mod = _mod_call(c, w_cond[layer], b_cond[layer]).reshape(bsz, N_MOD, d)
        npre = norm_pre[layer][:, None, :]
        npost = norm_post[layer][:, None, :]
        x = _ffn_call(x, mod, npre[0], npost[0], w_ffn_in[layer], w_ffn_out[layer],
                      which=0, tm=tm_ffn)
        x = _mixer_call(x, mod, npre[1], npost[1], w_mix_in[layer].astype(BF16),
                        w_mix_out[layer].astype(BF16), cos_t, sin_t, attn_sinks[layer],
                        hgrn_lb_logits, hgrn_gnorm[layer][None, :], tm=tm_mix, layer=layer)
        x = _ffn_call(x, mod, npre[2], npost[2], w_ffn_in[layer], w_ffn_out[layer],
                      which=1, tm=tm_ffn)
    return x
```

```python
import functools

import jax
import jax.numpy as jnp
from jax import lax
from jax.experimental import pallas as pl
from jax.experimental.pallas import tpu as pltpu

F32 = jnp.float32
BF16 = jnp.bfloat16

EPS = 1e-6
NEG_INF = -1e30
ROPE_THETA = 10000.0
LOG2E = 1.4426950408889634

CHUNK = 64
ATTN_HEAD_DIM = 64
ATTN_Q_HEADS = 8
ATTN_KV_HEADS = 2
ATTN_GROUP = ATTN_Q_HEADS // ATTN_KV_HEADS
ATTN_WIDTH = ATTN_Q_HEADS * ATTN_HEAD_DIM
KV_WIDTH = ATTN_KV_HEADS * ATTN_HEAD_DIM
HGRN_HEADS = 4
HGRN_DIM = 128
HGRN_WIDTH = HGRN_HEADS * HGRN_DIM
N_MOD = 9
FFN_RES_WEIGHT = 0.5

OFF_AQ = 0
OFF_AKV = OFF_AQ + ATTN_WIDTH
OFF_HQ = OFF_AKV + 2 * KV_WIDTH
OFF_HF = OFF_HQ + HGRN_WIDTH
OFF_HI = OFF_HF + HGRN_WIDTH
OFF_HG = OFF_HI + HGRN_WIDTH

LANES = 128
SUBLANES = 8
GROUP_ROWS = 2 * CHUNK
BAND_CHUNKS = 4
BAND_ROWS = BAND_CHUNKS * CHUNK
FFN_COL_TILE = 256
FFN_EDGE_ROWS = 512
MOD_COL_TILE = 2304
ROPE_FREQS = ATTN_HEAD_DIM // 2
HGRN_SPAN = 256
VMEM_LIMIT_BYTES = 56 * 1024 * 1024

FLAG_LANE = ATTN_HEAD_DIM + BAND_CHUNKS

DIRECT_SPAN = 4


def _sigmoid(v):
    return 1.0 / (1.0 + jnp.exp(-v))


def _silu(v):
    hv = 0.5 * v
    return hv + hv * jnp.tanh(hv)


def _rms(v):
    return v * lax.rsqrt(jnp.mean(v * v, axis=-1, keepdims=True) + EPS)


def _modulated(x, gain, shift):
    return _rms(x).astype(BF16) * gain.astype(BF16) + shift.astype(BF16)


def _dot(a, b):
    return jnp.dot(a, b, preferred_element_type=F32)


def _dot_nt(a, b):
    return lax.dot_general(a, b, (((1,), (1,)), ((), ())), preferred_element_type=F32)


def _dot_tn(a, b):
    return lax.dot_general(a, b, (((0,), (0,)), ((), ())), preferred_element_type=F32)


def _mod_kernel(c_ref, w_ref, b_ref, o_ref):
    c = c_ref[...]
    ca = (c * _sigmoid(c)).astype(BF16)
    o_ref[...] = _dot(ca, w_ref[...].astype(BF16)) + b_ref[...]


def _mod_call(c, w, b):
    bsz, d = c.shape
    n = w.shape[1]
    tn = MOD_COL_TILE
    return pl.pallas_call(
        _mod_kernel,
        grid=(n // tn,),
        in_specs=[
            pl.BlockSpec((bsz, d), lambda j: (0, 0)),
            pl.BlockSpec((d, tn), lambda j: (0, j)),
            pl.BlockSpec((1, tn), lambda j: (0, j)),
        ],
        out_specs=pl.BlockSpec((bsz, tn), lambda j: (0, j)),
        out_shape=jax.ShapeDtypeStruct((bsz, n), F32),
        compiler_params=pltpu.CompilerParams(dimension_semantics=("arbitrary",)),
        name="mod",
    )(c, w, b.reshape(1, n))


def _rope_kernel(pos_ref, inv_ref, cos_ref, sin_ref):
    tr = pos_ref.shape[0]
    ang = pos_ref[...].astype(F32) * inv_ref[...]
    group = lax.broadcasted_iota(jnp.int32, (tr, LANES), 1) // ROPE_FREQS
    sign = jnp.where(group % 2 == 0, -1.0, 1.0)
    n_grp = LANES // ROPE_FREQS
    for tab, out_ref, signed in ((jnp.cos(ang), cos_ref, False), (jnp.sin(ang), sin_ref, True)):
        rolled = [tab] + [pltpu.roll(tab, ROPE_FREQS * k, 1) for k in range(1, n_grp)]
        for a in range(n_grp):
            piece = rolled[(0 - a) % n_grp]
            for kk in range(1, n_grp):
                piece = jnp.where(group == kk, rolled[(kk - a) % n_grp], piece)
            if signed:
                piece = piece * sign
            out_ref[pl.ds(a, tr, stride=n_grp), :] = piece


def _rope_call(pos_rep, inv_row):
    rows = pos_rep.shape[0]
    n_grp = LANES // ROPE_FREQS
    tr = min(rows, 512)
    out_spec = pl.BlockSpec((n_grp * tr, LANES), lambda i: (i, 0))
    return pl.pallas_call(
        _rope_kernel,
        grid=(rows // tr,),
        in_specs=[pl.BlockSpec((tr, LANES), lambda i: (i, 0)),
                  pl.BlockSpec((1, LANES), lambda i: (0, 0))],
        out_specs=[out_spec, out_spec],
        out_shape=[jax.ShapeDtypeStruct((n_grp * rows, LANES), F32)] * 2,
        compiler_params=pltpu.CompilerParams(dimension_semantics=("arbitrary",)),
        name="rope",
    )(pos_rep, inv_row)


def _resident(shape, index=None):
    index = (0,) * len(shape) if index is None else index
    return pl.BlockSpec(shape, lambda b, s: index, pipeline_mode=pl.Buffered(1))


def _ffn_kernel(x_ref, mod_ref, npre_ref, npost_ref, win_ref, wout_ref, o_ref, *, sub, d_ff):
    sh = mod_ref[0, 3 * sub:3 * sub + 1, :]
    sc = mod_ref[0, 3 * sub + 1:3 * sub + 2, :]
    gt = mod_ref[0, 3 * sub + 2:3 * sub + 3, :]
    tm = x_ref.shape[1]
    n_chunks = d_ff // FFN_COL_TILE
    gain = npre_ref[...] * (1.0 + sc)
    out_gain = (FFN_RES_WEIGHT * gt) * npost_ref[...]

    def chunk(h_rows, j):
        lo = j * FFN_COL_TILE
        g = _dot(h_rows, win_ref[0, :, lo:lo + FFN_COL_TILE])
        u = _dot(h_rows, win_ref[0, :, d_ff + lo:d_ff + lo + FFN_COL_TILE])
        a = (_silu(g) * u).astype(BF16)
        return _dot(a, wout_ref[0, lo:lo + FFN_COL_TILE, :])

    slabs = [slice(r, r + FFN_EDGE_ROWS) for r in range(0, tm, FFN_EDGE_ROWS)]
    h_parts, acc_parts = [], []
    for rows in slabs:
        h_rows = _modulated(x_ref[0, rows, :], gain, sh)
        h_parts.append(h_rows)
        acc_parts.append(chunk(h_rows, 0))
    h = jnp.concatenate(h_parts, axis=0)
    acc = jnp.concatenate(acc_parts, axis=0)
    for j in range(1, n_chunks - 1):
        acc = acc + chunk(h, j)
    for rows in slabs:
        acc_rows = acc[rows, :] + chunk(h[rows, :], n_chunks - 1)
        o_ref[0, rows, :] = x_ref[0, rows, :] + _rms(acc_rows) * out_gain


def _ffn_call(x, mod, npre, npost, w_in, w_out, *, which, tm):
    bsz, seq, d = x.shape
    d_ff = w_out.shape[1]
    sub = 2 * which
    xspec = pl.BlockSpec((1, tm, d), lambda b, s: (b, s, 0))
    return pl.pallas_call(
        functools.partial(_ffn_kernel, sub=sub, d_ff=d_ff),
        grid=(bsz, seq // tm),
        in_specs=[
            xspec,
            pl.BlockSpec((1, N_MOD, d), lambda b, s: (b, 0, 0)),
            _resident((1, d)),
            _resident((1, d)),
            _resident((1, d, 2 * d_ff), (which, 0, 0)),
            _resident((1, d_ff, d), (which, 0, 0)),
        ],
        out_specs=xspec,
        out_shape=jax.ShapeDtypeStruct(x.shape, F32),
        compiler_params=pltpu.CompilerParams(
            dimension_semantics=("arbitrary", "arbitrary"),
            vmem_limit_bytes=VMEM_LIMIT_BYTES),
        name="ffn%d" % sub,
    )(x, mod, npre, npost, w_in, w_out)


def _mixer_kernel(sinks_ref, x_ref, mod_ref, npre_ref, npost_ref, win_ref, wout_ref,
                  cos_ref, sin_ref, lbl_ref, gn_ref, tril_ref, o_ref,
                  kbuf, vbuf, st_ref, mix_scr, b_scr, *, tm, layer):
    si = pl.program_id(1)
    n_groups = tm // GROUP_ROWS
    n_spans = tm // HGRN_SPAN

    lane = lax.broadcasted_iota(jnp.int32, (tm, LANES), 1)
    row = lax.broadcasted_iota(jnp.int32, (tm, LANES), 0)
    low_head = lane < ATTN_HEAD_DIM

    @pl.when(si == 0)
    def _():
        carry_lane = lax.broadcasted_iota(jnp.int32, (GROUP_ROWS, LANES), 1)
        k0 = jnp.where(carry_lane == FLAG_LANE, 1.0, 0.0).astype(BF16)
        v0 = jnp.where(carry_lane < ATTN_HEAD_DIM, 0.0, 1.0).astype(BF16)
        for hk in range(ATTN_KV_HEADS):
            kbuf[hk, 0:GROUP_ROWS, :] = k0
            vbuf[hk, 0:GROUP_ROWS, :] = v0
        st_ref[...] = jnp.zeros(st_ref.shape, F32)

    x = x_ref[0]
    sh = mod_ref[0, 3:4, :]
    sc = mod_ref[0, 4:5, :]
    gt = mod_ref[0, 5:6, :]
    h = _modulated(x, npre_ref[...] * (1.0 + sc), sh)

    cos = cos_ref[0]
    sin = sin_ref[0]
    first_half = (lane % ATTN_HEAD_DIM) < (ATTN_HEAD_DIM // 2)
    chunk_id = (row // CHUNK) % BAND_CHUNKS
    key_extra = jnp.where(lane == ATTN_HEAD_DIM + chunk_id, 1.0, 0.0)
    hidden = ATTN_HEAD_DIM + (chunk_id + 1) % BAND_CHUNKS
    query_extra = jnp.where((lane == hidden) | (lane == FLAG_LANE), NEG_INF, 0.0)

    def rope(t):
        rot = jnp.where(first_half, pltpu.roll(t, LANES - 32, 1), pltpu.roll(t, 32, 1))
        return t * cos + rot * sin

    aq = _dot(h, win_ref[:, OFF_AQ:OFF_AQ + ATTN_WIDTH])
    akv = _dot(h, win_ref[:, OFF_AKV:OFF_AKV + 2 * KV_WIDTH])
    k = rope(akv[:, 0:KV_WIDTH])
    v = akv[:, KV_WIDTH:2 * KV_WIDTH]
    for src, buf, extra in ((k, kbuf, key_extra), (v, vbuf, 1.0)):
        buf[0, GROUP_ROWS:GROUP_ROWS + tm, :] = jnp.where(low_head, src, extra).astype(BF16)
        buf[1, GROUP_ROWS:GROUP_ROWS + tm, :] = jnp.where(
            low_head, pltpu.roll(src, ATTN_HEAD_DIM, 1), extra).astype(BF16)

    qh = []
    for jb in range(ATTN_Q_HEADS // 2):
        qb = rope(aq[:, jb * LANES:(jb + 1) * LANES]) * (ATTN_HEAD_DIM ** -0.5 * LOG2E)
        qh.append(jnp.where(low_head, qb, query_extra).astype(BF16))
        qh.append(jnp.where(low_head, pltpu.roll(qb, ATTN_HEAD_DIM, 1), query_extra).astype(BF16))

    low_grp = lax.broadcasted_iota(jnp.int32, (GROUP_ROWS, LANES), 1) < ATTN_HEAD_DIM

    def attn_block(g, hk):
        r0 = g * GROUP_ROWS
        heads = [hk * ATTN_GROUP + i for i in range(ATTN_GROUP)]
        qs = jnp.concatenate([qh[hd][r0:r0 + GROUP_ROWS, :] for hd in heads], axis=0)
        kband = kbuf[hk, r0:r0 + BAND_ROWS, :]
        vband = vbuf[hk, r0:r0 + BAND_ROWS, :]
        s = _dot_nt(qs, kband)
        m = jnp.max(s, axis=-1, keepdims=True)
        p = jnp.exp2(s - m)
        o = _dot(p.astype(BF16), vband)
        for pair in range(ATTN_GROUP // 2):
            res = []
            for odd in range(2):
                a0 = (2 * pair + odd) * GROUP_ROWS
                o_h = o[a0:a0 + GROUP_ROWS, :]
                sink_term = jnp.exp2(sinks_ref[heads[2 * pair + odd]] * LOG2E
                                     - m[a0:a0 + GROUP_ROWS, :])
                swapped = pltpu.roll(o_h, ATTN_HEAD_DIM, 1)
                if odd:
                    res.append(swapped / (o_h + sink_term))
                else:
                    res.append(o_h / (swapped + sink_term))
            col = (hk * (ATTN_GROUP // 2) + pair) * LANES
            mix_scr[r0:r0 + GROUP_ROWS, col:col + LANES] = (
                jnp.where(low_grp, res[0], res[1]).astype(BF16))

    lbl = lbl_ref[...]
    e_lb = jnp.exp(lbl - jnp.max(lbl, axis=0, keepdims=True))
    lb_all = jnp.sum(e_lb[0:layer + 1, :], axis=0, keepdims=True) / jnp.sum(e_lb, axis=0, keepdims=True)

    hq = _dot(h, win_ref[:, OFF_HQ:OFF_HQ + HGRN_WIDTH])
    hf = _dot(h, win_ref[:, OFF_HF:OFF_HF + HGRN_WIDTH])
    hi = _dot(h, win_ref[:, OFF_HI:OFF_HI + HGRN_WIDTH])
    hg = _dot(h, win_ref[:, OFF_HG:OFF_HG + HGRN_WIDTH])

    f = lb_all + (1.0 - lb_all) * _sigmoid(hf)
    one_minus_f = 1.0 - f
    g = jnp.log2(f)
    g1 = g.astype(BF16)
    g2 = (g - g1.astype(F32)).astype(BF16)
    tril = tril_ref[...]
    for sp in range(n_spans):
        rows = slice(sp * HGRN_SPAN, (sp + 1) * HGRN_SPAN)
        b_scr[rows, :] = _dot(tril, g1[rows, :]) + _dot(tril, g2[rows, :])

    half_span = HGRN_SPAN // 2
    level_halves = []
    width = half_span
    while width >= DIRECT_SPAN:
        level_halves.append(width)
        width //= 2
    ti = lax.broadcasted_iota(jnp.int32, (half_span, half_span), 0)
    sj = lax.broadcasted_iota(jnp.int32, (half_span, half_span), 1)
    quad_mask = {}
    for half in level_halves[1:]:
        blk = 2 * half
        quad_mask[half] = ((ti // blk) == (sj // blk)) & ((ti % blk) >= half) & ((sj % blk) < half)
    diag = [sj == ti - delta for delta in range(DIRECT_SPAN)]
    sub_row = lax.broadcasted_iota(jnp.int32, (HGRN_SPAN, LANES), 0) % SUBLANES
    later4 = (sub_row % (2 * DIRECT_SPAN)) >= DIRECT_SPAN
    in_vreg = lax.broadcasted_iota(jnp.int32, (1, SUBLANES, LANES), 1)
    direct_ok = [(in_vreg % DIRECT_SPAN) >= delta for delta in range(DIRECT_SPAN)]

    gn = gn_ref[...]

    def hgrn_unit(hd, sp):
        cols = slice(hd * HGRN_DIM, (hd + 1) * HGRN_DIM)
        base = sp * HGRN_SPAN
        rows = slice(base, base + HGRN_SPAN)
        rq = _silu(hq[rows, cols]) * (HGRN_DIM ** -0.5)
        rk = one_minus_f[rows, cols]
        rv = hi[rows, cols].astype(BF16)
        gate = _silu(hg[rows, cols])
        b = b_scr[rows, cols]

        lvl_w = {}
        for half in level_halves:
            blk = 2 * half
            expo, operand = [], []
            for mblk in range(HGRN_SPAN // blk):
                lo = mblk * blk
                ref_row = b_scr[pl.ds(base + lo + half - 1, 1), cols]
                if half >= SUBLANES:
                    ref_half = jnp.broadcast_to(ref_row, (half, HGRN_DIM))
                    expo += [ref_half - b[lo:lo + half, :], b[lo + half:lo + blk, :] - ref_half]
                    operand += [rk[lo:lo + half, :], rq[lo + half:lo + blk, :]]
                else:
                    expo.append(jnp.broadcast_to(ref_row, (blk, HGRN_DIM)))
            if half >= SUBLANES:
                w = jnp.concatenate(operand, axis=0) * jnp.exp2(jnp.concatenate(expo, axis=0))
            else:
                dist = jnp.abs(b - jnp.concatenate(expo, axis=0))
                w = jnp.where(later4, rq, rk) * jnp.exp2(-dist)
            lvl_w[half] = w.astype(BF16)

        def grouped(t):
            return t.reshape(HGRN_SPAN // SUBLANES, SUBLANES, HGRN_DIM)

        b3, rq3, rk3 = grouped(b), grouped(rq), grouped(rk)
        direct = []
        for delta in range(DIRECT_SPAN):
            if delta == 0:
                prod = rq * rk
            else:
                gap = jnp.where(direct_ok[delta], b3 - pltpu.roll(b3, delta, 1), NEG_INF)
                prod = (rq3 * pltpu.roll(rk3, delta, 1) * jnp.exp2(gap)).reshape(HGRN_SPAN, HGRN_DIM)
            direct.append(jnp.sum(prod, axis=-1, keepdims=True))

        top = _dot_nt(lvl_w[half_span][half_span:, :], lvl_w[half_span][0:half_span, :])
        quads = []
        for qd in range(2):
            o0 = qd * half_span
            sc_q = jnp.zeros((half_span, half_span), F32)
            for delta in range(DIRECT_SPAN):
                sc_q = jnp.where(diag[delta], direct[delta][o0:o0 + half_span, :], sc_q)
            for half in level_halves[1:]:
                w_q = lvl_w[half][o0:o0 + half_span, :]
                sc_q = jnp.where(quad_mask[half], _dot_nt(w_q, w_q), sc_q)
            quads.append(sc_q.astype(BF16))
        s_hi = jnp.concatenate([top.astype(BF16), quads[1]], axis=1)

        st = st_ref[hd]
        b_last = b[HGRN_SPAN - 1:HGRN_SPAN, :]
        q_in = (rq * jnp.exp2(b)).astype(BF16)
        k_out = (rk * jnp.exp2(b_last - b)).astype(BF16)
        o = _dot_nt(q_in, st.astype(BF16)) + jnp.concatenate(
            [_dot(quads[0], rv[0:half_span, :]), _dot(s_hi, rv)], axis=0)
        st_ref[hd] = st * jnp.exp2(b_last) + _dot_tn(rv, k_out)
        o = _rms(o) * gn * gate
        mix_scr[rows, ATTN_WIDTH + hd * HGRN_DIM:ATTN_WIDTH + (hd + 1) * HGRN_DIM] = o.astype(BF16)

    attn_blocks = [(g, hk) for g in range(n_groups) for hk in range(ATTN_KV_HEADS)]
    hgrn_units = [(hd, sp) for sp in range(n_spans) for hd in range(HGRN_HEADS)]
    for i in range(max(len(attn_blocks), len(hgrn_units))):
        if i < len(attn_blocks):
            attn_block(*attn_blocks[i])
        if i < len(hgrn_units):
            hgrn_unit(*hgrn_units[i])

    for hk in range(ATTN_KV_HEADS):
        kbuf[hk, 0:GROUP_ROWS, :] = kbuf[hk, tm:tm + GROUP_ROWS, :]
        vbuf[hk, 0:GROUP_ROWS, :] = vbuf[hk, tm:tm + GROUP_ROWS, :]

    y = _dot(mix_scr[...], wout_ref[...])
    o_ref[0] = x_ref[0] + _rms(y) * (gt * npost_ref[...])


def _mixer_call(x, mod, npre, npost, w_in, w_out, cos_t, sin_t, sinks, lb_logits, gnorm, *, tm, layer):
    bsz, seq, d = x.shape
    d_proj = w_in.shape[1]
    d_mix = w_out.shape[0]
    assert tm % (BAND_CHUNKS * CHUNK) == 0 and tm % HGRN_SPAN == 0
    r = jnp.arange(HGRN_SPAN)
    tril = (r[None, :] <= r[:, None]).astype(BF16)
    xspec = pl.BlockSpec((1, tm, d), lambda b, s: (b, s, 0))
    tspec = pl.BlockSpec((1, tm, LANES), lambda b, s: (b, s, 0))
    return pl.pallas_call(
        functools.partial(_mixer_kernel, tm=tm, layer=layer),
        grid=(bsz, seq // tm),
        in_specs=[
            pl.BlockSpec(memory_space=pltpu.SMEM),
            xspec,
            pl.BlockSpec((1, N_MOD, d), lambda b, s: (b, 0, 0)),
            _resident((1, d)),
            _resident((1, d)),
            _resident((d, d_proj)),
            _resident((d_mix, d)),
            tspec,
            tspec,
            _resident(lb_logits.shape),
            _resident((1, HGRN_DIM)),
            _resident((HGRN_SPAN, HGRN_SPAN)),
        ],
        out_specs=xspec,
        out_shape=jax.ShapeDtypeStruct(x.shape, F32),
        scratch_shapes=[
            pltpu.VMEM((ATTN_KV_HEADS, GROUP_ROWS + tm, LANES), BF16),
            pltpu.VMEM((ATTN_KV_HEADS, GROUP_ROWS + tm, LANES), BF16),
            pltpu.VMEM((HGRN_HEADS, HGRN_DIM, HGRN_DIM), F32),
            pltpu.VMEM((tm, d_mix), BF16),
            pltpu.VMEM((tm, HGRN_WIDTH), F32),
        ],
        compiler_params=pltpu.CompilerParams(
            dimension_semantics=("arbitrary", "arbitrary"),
            vmem_limit_bytes=VMEM_LIMIT_BYTES),
        name="mixer",
    )(sinks, x, mod, npre, npost, w_in, w_out, cos_t, sin_t, lb_logits, gnorm, tril)


def kernel(x, c, positions, w_cond, b_cond, norm_pre, norm_post, ffn_w_in, ffn_w_out,
           w_mix_in, w_mix_out, attn_sinks, hgrn_lb_logits, hgrn_gnorm):
    bsz, seq, d = x.shape
    depth = w_cond.shape[0]
    tm_ffn = min(seq, 1024)
    tm_mix = min(seq, 512)

    inv_freq = 1.0 / (ROPE_THETA ** (jnp.arange(0, ATTN_HEAD_DIM, 2, dtype=F32) / ATTN_HEAD_DIM))
    inv_row = jnp.tile(inv_freq, LANES // ROPE_FREQS)[None, :]
    pos_rep = jnp.broadcast_to(positions.reshape(-1, 1), (bsz * seq, ROPE_FREQS)).reshape(-1, LANES)
    cos_t, sin_t = _rope_call(pos_rep, inv_row)
    cos_t = cos_t.reshape(bsz, seq, LANES)
    sin_t = sin_t.reshape(bsz, seq, LANES)

    w_ffn_in = ffn_w_in.astype(BF16)
    w_ffn_out = ffn_w_out.astype(BF16)
    for layer in range(depth):
        mod = _mod_call(c, w_cond[layer], b_cond[layer]).reshape(bsz, N_MOD, d)
        npre = norm_pre[layer][:, None, :]
        npost = norm_post[layer][:, None, :]
        x = _ffn_call(x, mod, npre[0], npost[0], w_ffn_in[layer], w_ffn_out[layer],
                      which=0, tm=tm_ffn)
        x = _mixer_call(x, mod, npre[1], npost[1], w_mix_in[layer].astype(BF16),
                        w_mix_out[layer].astype(BF16), cos_t, sin_t, attn_sinks[layer],
                        hgrn_lb_logits, hgrn_gnorm[layer][None, :], tm=tm_mix, layer=layer)
        x = _ffn_call(x, mod, npre[2], npost[2], w_ffn_in[layer], w_ffn_out[layer],
                      which=1, tm=tm_ffn)
    return x
```

```python
import functools

import jax
import jax.numpy as jnp
from jax import lax
from jax.experimental import pallas as pl
from jax.experimental.pallas import tpu as pltpu

F32 = jnp.float32
BF16 = jnp.bfloat16

EPS = 1e-6
NEG_INF = -1e30
ROPE_THETA = 10000.0
LOG2E = 1.4426950408889634

CHUNK = 64
ATTN_HEAD_DIM = 64
ATTN_Q_HEADS = 8
ATTN_KV_HEADS = 2
ATTN_GROUP = ATTN_Q_HEADS // ATTN_KV_HEADS
ATTN_WIDTH = ATTN_Q_HEADS * ATTN_HEAD_DIM
KV_WIDTH = ATTN_KV_HEADS * ATTN_HEAD_DIM
HGRN_HEADS = 4
HGRN_DIM = 128
HGRN_WIDTH = HGRN_HEADS * HGRN_DIM
N_MOD = 9
FFN_RES_WEIGHT = 0.5

OFF_AQ = 0
OFF_AKV = OFF_AQ + ATTN_WIDTH
OFF_HQ = OFF_AKV + 2 * KV_WIDTH
OFF_HF = OFF_HQ + HGRN_WIDTH
OFF_HI = OFF_HF + HGRN_WIDTH
OFF_HG = OFF_HI + HGRN_WIDTH

LANES = 128
SUBLANES = 8
GROUP_ROWS = 2 * CHUNK
BAND_CHUNKS = 4
BAND_ROWS = BAND_CHUNKS * CHUNK
FFN_COL_TILE = 256
FFN_EDGE_ROWS = 512
MOD_COL_TILE = 2304
ROPE_FREQS = ATTN_HEAD_DIM // 2
HGRN_SPAN = 256
VMEM_LIMIT_BYTES = 56 * 1024 * 1024

FLAG_LANE = ATTN_HEAD_DIM + BAND_CHUNKS

DIRECT_SPAN = 4


def _sigmoid(v):
    return 1.0 / (1.0 + jnp.exp(-v))


def _silu(v):
    hv = 0.5 * v
    return hv + hv * jnp.tanh(hv)


def _rms(v):
    return v * lax.rsqrt(jnp.mean(v * v, axis=-1, keepdims=True) + EPS)


def _modulated(x, gain, shift):
    return _rms(x).astype(BF16) * gain.astype(BF16) + shift.astype(BF16)


def _dot(a, b):
    return jnp.dot(a, b, preferred_element_type=F32)


def _dot_nt(a, b):
    return lax.dot_general(a, b, (((1,), (1,)), ((), ())), preferred_element_type=F32)


def _dot_tn(a, b):
    return lax.dot_general(a, b, (((0,), (0,)), ((), ())), preferred_element_type=F32)


def _mod_kernel(c_ref, w_ref, b_ref, o_ref):
    c = c_ref[...]
    ca = (c * _sigmoid(c)).astype(BF16)
    o_ref[...] = _dot(ca, w_ref[...].astype(BF16)) + b_ref[...]


def _mod_call(c, w, b):
    bsz, d = c.shape
    n = w.shape[1]
    tn = MOD_COL_TILE
    return pl.pallas_call(
        _mod_kernel,
        grid=(n // tn,),
        in_specs=[
            pl.BlockSpec((bsz, d), lambda j: (0, 0)),
            pl.BlockSpec((d, tn), lambda j: (0, j)),
            pl.BlockSpec((1, tn), lambda j: (0, j)),
        ],
        out_specs=pl.BlockSpec((bsz, tn), lambda j: (0, j)),
        out_shape=jax.ShapeDtypeStruct((bsz, n), F32),
        compiler_params=pltpu.CompilerParams(dimension_semantics=("arbitrary",)),
        name="mod",
    )(c, w, b.reshape(1, n))


def _rope_kernel(pos_ref, inv_ref, cos_ref, sin_ref):
    tr = pos_ref.shape[0]
    ang = pos_ref[...].astype(F32) * inv_ref[...]
    group = lax.broadcasted_iota(jnp.int32, (tr, LANES), 1) // ROPE_FREQS
    sign = jnp.where(group % 2 == 0, -1.0, 1.0)
    n_grp = LANES // ROPE_FREQS
    for tab, out_ref, signed in ((jnp.cos(ang), cos_ref, False), (jnp.sin(ang), sin_ref, True)):
        rolled = [tab] + [pltpu.roll(tab, ROPE_FREQS * k, 1) for k in range(1, n_grp)]
        for a in range(n_grp):
            piece = rolled[(0 - a) % n_grp]
            for kk in range(1, n_grp):
                piece = jnp.where(group == kk, rolled[(kk - a) % n_grp], piece)
            if signed:
                piece = piece * sign
            out_ref[pl.ds(a, tr, stride=n_grp), :] = piece


def _rope_call(pos_rep, inv_row):
    rows = pos_rep.shape[0]
    n_grp = LANES // ROPE_FREQS
    tr = min(rows, 512)
    out_spec = pl.BlockSpec((n_grp * tr, LANES), lambda i: (i, 0))
    return pl.pallas_call(
        _rope_kernel,
        grid=(rows // tr,),
        in_specs=[pl.BlockSpec((tr, LANES), lambda i: (i, 0)),
                  pl.BlockSpec((1, LANES), lambda i: (0, 0))],
        out_specs=[out_spec, out_spec],
        out_shape=[jax.ShapeDtypeStruct((n_grp * rows, LANES), F32)] * 2,
        compiler_params=pltpu.CompilerParams(dimension_semantics=("arbitrary",)),
        name="rope",
    )(pos_rep, inv_row)


def _resident(shape, index=None):
    index = (0,) * len(shape) if index is None else index
    return pl.BlockSpec(shape, lambda b, s: index, pipeline_mode=pl.Buffered(1))


def _ffn_kernel(x_ref, mod_ref, npre_ref, npost_ref, win_ref, wout_ref, o_ref, *, sub, d_ff):
    sh = mod_ref[0, 3 * sub:3 * sub + 1, :]
    sc = mod_ref[0, 3 * sub + 1:3 * sub + 2, :]
    gt = mod_ref[0, 3 * sub + 2:3 * sub + 3, :]
    tm = x_ref.shape[1]
    n_chunks = d_ff // FFN_COL_TILE
    gain = npre_ref[...] * (1.0 + sc)
    out_gain = (FFN_RES_WEIGHT * gt) * npost_ref[...]

    def chunk(h_rows, j):
        lo = j * FFN_COL_TILE
        g = _dot(h_rows, win_ref[0, :, lo:lo + FFN_COL_TILE])
        u = _dot(h_rows, win_ref[0, :, d_ff + lo:d_ff + lo + FFN_COL_TILE])
        a = (_silu(g) * u).astype(BF16)
        return _dot(a, wout_ref[0, lo:lo + FFN_COL_TILE, :])

    slabs = [slice(r, r + FFN_EDGE_ROWS) for r in range(0, tm, FFN_EDGE_ROWS)]
    h_parts, acc_parts = [], []
    for rows in slabs:
        h_rows = _modulated(x_ref[0, rows, :], gain, sh)
        h_parts.append(h_rows)
        acc_parts.append(chunk(h_rows, 0))
    h = jnp.concatenate(h_parts, axis=0)
    acc = jnp.concatenate(acc_parts, axis=0)
    for j in range(1, n_chunks - 1):
        acc = acc + chunk(h, j)
    for rows in slabs:
        acc_rows = acc[rows, :] + chunk(h[rows, :], n_chunks - 1)
        o_ref[0, rows, :] = x_ref[0, rows, :] + _rms(acc_rows) * out_gain


def _ffn_call(x, mod, npre, npost, w_in, w_out, *, which, tm):
    bsz, seq, d = x.shape
    d_ff = w_out.shape[1]
    sub = 2 * which
    xspec = pl.BlockSpec((1, tm, d), lambda b, s: (b, s, 0))
    return pl.pallas_call(
        functools.partial(_ffn_kernel, sub=sub, d_ff=d_ff),
        grid=(bsz, seq // tm),
        in_specs=[
            xspec,
            pl.BlockSpec((1, N_MOD, d), lambda b, s: (b, 0, 0)),
            _resident((1, d)),
            _resident((1, d)),
            _resident((1, d, 2 * d_ff), (which, 0, 0)),
            _resident((1, d_ff, d), (which, 0, 0)),
        ],
        out_specs=xspec,
        out_shape=jax.ShapeDtypeStruct(x.shape, F32),
        compiler_params=pltpu.CompilerParams(
            dimension_semantics=("arbitrary", "arbitrary"),
            vmem_limit_bytes=VMEM_LIMIT_BYTES),
        name="ffn%d" % sub,
    )(x, mod, npre, npost, w_in, w_out)


def _mixer_kernel(sinks_ref, x_ref, mod_ref, npre_ref, npost_ref, win_ref, wout_ref,
                  cos_ref, sin_ref, lbl_ref, gn_ref, tril_ref, o_ref,
                  kbuf, vbuf, st_ref, mix_scr, b_scr, *, tm, layer):
    si = pl.program_id(1)
    n_groups = tm // GROUP_ROWS
    n_spans = tm // HGRN_SPAN

    lane = lax.broadcasted_iota(jnp.int32, (tm, LANES), 1)
    row = lax.broadcasted_iota(jnp.int32, (tm, LANES), 0)
    low_head = lane < ATTN_HEAD_DIM

    @pl.when(si == 0)
    def _():
        carry_lane = lax.broadcasted_iota(jnp.int32, (GROUP_ROWS, LANES), 1)
        k0 = jnp.where(carry_lane == FLAG_LANE, 1.0, 0.0).astype(BF16)
        v0 = jnp.where(carry_lane < ATTN_HEAD_DIM, 0.0, 1.0).astype(BF16)
        for hk in range(ATTN_KV_HEADS):
            kbuf[hk, 0:GROUP_ROWS, :] = k0
            vbuf[hk, 0:GROUP_ROWS, :] = v0
        st_ref[...] = jnp.zeros(st_ref.shape, F32)

    x = x_ref[0]
    sh = mod_ref[0, 3:4, :]
    sc = mod_ref[0, 4:5, :]
    gt = mod_ref[0, 5:6, :]
    h = _modulated(x, npre_ref[...] * (1.0 + sc), sh)

    cos = cos_ref[0]
    sin = sin_ref[0]
    first_half = (lane % ATTN_HEAD_DIM) < (ATTN_HEAD_DIM // 2)
    chunk_id = (row // CHUNK) % BAND_CHUNKS
    key_extra = jnp.where(lane == ATTN_HEAD_DIM + chunk_id, 1.0, 0.0)
    hidden = ATTN_HEAD_DIM + (chunk_id + 1) % BAND_CHUNKS
    query_extra = jnp.where((lane == hidden) | (lane == FLAG_LANE), NEG_INF, 0.0)

    def rope(t):
        rot = jnp.where(first_half, pltpu.roll(t, LANES - 32, 1), pltpu.roll(t, 32, 1))
        return t * cos + rot * sin

    aq = _dot(h, win_ref[:, OFF_AQ:OFF_AQ + ATTN_WIDTH])
    akv = _dot(h, win_ref[:, OFF_AKV:OFF_AKV + 2 * KV_WIDTH])
    k = rope(akv[:, 0:KV_WIDTH])
    v = akv[:, KV_WIDTH:2 * KV_WIDTH]
    for src, buf, extra in ((k, kbuf, key_extra), (v, vbuf, 1.0)):
        buf[0, GROUP_ROWS:GROUP_ROWS + tm, :] = jnp.where(low_head, src, extra).astype(BF16)
        buf[1, GROUP_ROWS:GROUP_ROWS + tm, :] = jnp.where(
            low_head, pltpu.roll(src, ATTN_HEAD_DIM, 1), extra).astype(BF16)

    qh = []
    for jb in range(ATTN_Q_HEADS // 2):
        qb = rope(aq[:, jb * LANES:(jb + 1) * LANES]) * (ATTN_HEAD_DIM ** -0.5 * LOG2E)
        qh.append(jnp.where(low_head, qb, query_extra).astype(BF16))
        qh.append(jnp.where(low_head, pltpu.roll(qb, ATTN_HEAD_DIM, 1), query_extra).astype(BF16))

    low_grp = lax.broadcasted_iota(jnp.int32, (GROUP_ROWS, LANES), 1) < ATTN_HEAD_DIM

    def attn_block(g, hk):
        r0 = g * GROUP_ROWS
        heads = [hk * ATTN_GROUP + i for i in range(ATTN_GROUP)]
        qs = jnp.concatenate([qh[hd][r0:r0 + GROUP_ROWS, :] for hd in heads], axis=0)
        kband = kbuf[hk, r0:r0 + BAND_ROWS, :]
        vband = vbuf[hk, r0:r0 + BAND_ROWS, :]
        s = _dot_nt(qs, kband)
        m = jnp.max(s, axis=-1, keepdims=True)
        p = jnp.exp2(s - m)
        o = _dot(p.astype(BF16), vband)
        for pair in range(ATTN_GROUP // 2):
            res = []
            for odd in range(2):
                a0 = (2 * pair + odd) * GROUP_ROWS
                o_h = o[a0:a0 + GROUP_ROWS, :]
                sink_term = jnp.exp2(sinks_ref[heads[2 * pair + odd]] * LOG2E
                                     - m[a0:a0 + GROUP_ROWS, :])
                swapped = pltpu.roll(o_h, ATTN_HEAD_DIM, 1)
                if odd:
                    res.append(swapped / (o_h + sink_term))
                else:
                    res.append(o_h / (swapped + sink_term))
            col = (hk * (ATTN_GROUP // 2) + pair) * LANES
            mix_scr[r0:r0 + GROUP_ROWS, col:col + LANES] = (
                jnp.where(low_grp, res[0], res[1]).astype(BF16))

    lbl = lbl_ref[...]
    e_lb = jnp.exp(lbl - jnp.max(lbl, axis=0, keepdims=True))
    lb_all = jnp.sum(e_lb[0:layer + 1, :], axis=0, keepdims=True) / jnp.sum(e_lb, axis=0, keepdims=True)

    hq = _dot(h, win_ref[:, OFF_HQ:OFF_HQ + HGRN_WIDTH])
    hf = _dot(h, win_ref[:, OFF_HF:OFF_HF + HGRN_WIDTH])
    hi = _dot(h, win_ref[:, OFF_HI:OFF_HI + HGRN_WIDTH])
    hg = _dot(h, win_ref[:, OFF_HG:OFF_HG + HGRN_WIDTH])

    tril = tril_ref[...]

    half_span = HGRN_SPAN // 2
    level_halves = []
    width = half_span
    while width >= DIRECT_SPAN:
        level_halves.append(width)
        width //= 2
    ti = lax.broadcasted_iota(jnp.int32, (half_span, half_span), 0)
    sj = lax.broadcasted_iota(jnp.int32, (half_span, half_span), 1)
    quad_mask = {}
    for half in level_halves[1:]:
        blk = 2 * half
        quad_mask[half] = ((ti // blk) == (sj // blk)) & ((ti % blk) >= half) & ((sj % blk) < half)
    diag = [sj == ti - delta for delta in range(DIRECT_SPAN)]
    sub_row = lax.broadcasted_iota(jnp.int32, (HGRN_SPAN, LANES), 0) % SUBLANES
    later4 = (sub_row % (2 * DIRECT_SPAN)) >= DIRECT_SPAN
    in_vreg = lax.broadcasted_iota(jnp.int32, (1, SUBLANES, LANES), 1)
    direct_ok = [(in_vreg % DIRECT_SPAN) >= delta for delta in range(DIRECT_SPAN)]

    gn = gn_ref[...]

    def hgrn_unit(hd, sp):
        cols = slice(hd * HGRN_DIM, (hd + 1) * HGRN_DIM)
        base = sp * HGRN_SPAN
        rows = slice(base, base + HGRN_SPAN)
        rq = _silu(hq[rows, cols]) * (HGRN_DIM ** -0.5)
        lb = lb_all[:, cols]
        f = lb + (1.0 - lb) * _sigmoid(hf[rows, cols])
        rk = 1.0 - f
        rv = hi[rows, cols].astype(BF16)
        gate = _silu(hg[rows, cols])
        g = jnp.log2(f)
        g1 = g.astype(BF16)
        g2 = (g - g1.astype(F32)).astype(BF16)
        b = _dot(tril, jnp.concatenate([g1, g2], axis=1))
        b = b[:, 0:HGRN_DIM] + b[:, HGRN_DIM:]
        b_scr[rows, cols] = b

        lvl_w = {}
        for half in level_halves:
            blk = 2 * half
            expo, operand = [], []
            for mblk in range(HGRN_SPAN // blk):
                lo = mblk * blk
                ref_row = b_scr[pl.ds(base + lo + half - 1, 1), cols]
                if half >= SUBLANES:
                    ref_half = jnp.broadcast_to(ref_row, (half, HGRN_DIM))
                    expo += [ref_half - b[lo:lo + half, :], b[lo + half:lo + blk, :] - ref_half]
                    operand += [rk[lo:lo + half, :], rq[lo + half:lo + blk, :]]
                else:
                    expo.append(jnp.broadcast_to(ref_row, (blk, HGRN_DIM)))
            if half >= SUBLANES:
                w = jnp.concatenate(operand, axis=0) * jnp.exp2(jnp.concatenate(expo, axis=0))
            else:
                dist = jnp.abs(b - jnp.concatenate(expo, axis=0))
                w = jnp.where(later4, rq, rk) * jnp.exp2(-dist)
            lvl_w[half] = w.astype(BF16)

        def grouped(t):
            return t.reshape(HGRN_SPAN // SUBLANES, SUBLANES, HGRN_DIM)

        b3, rq3, rk3 = grouped(b), grouped(rq), grouped(rk)
        direct = []
        for delta in range(DIRECT_SPAN):
            if delta == 0:
                prod = rq * rk
            else:
                gap = jnp.where(direct_ok[delta], b3 - pltpu.roll(b3, delta, 1), NEG_INF)
                prod = (rq3 * pltpu.roll(rk3, delta, 1) * jnp.exp2(gap)).reshape(HGRN_SPAN, HGRN_DIM)
            direct.append(jnp.sum(prod, axis=-1, keepdims=True))

        top = _dot_nt(lvl_w[half_span][half_span:, :], lvl_w[half_span][0:half_span, :])
        quads = []
        for qd in range(2):
            o0 = qd * half_span
            sc_q = jnp.zeros((half_span, half_span), F32)
            for delta in range(DIRECT_SPAN):
                sc_q = jnp.where(diag[delta], direct[delta][o0:o0 + half_span, :], sc_q)
            for half in level_halves[1:]:
                w_q = lvl_w[half][o0:o0 + half_span, :]
                sc_q = jnp.where(quad_mask[half], _dot_nt(w_q, w_q), sc_q)
            quads.append(sc_q.astype(BF16))
        s_hi = jnp.concatenate([top.astype(BF16), quads[1]], axis=1)

        st = st_ref[hd]
        b_last = b[HGRN_SPAN - 1:HGRN_SPAN, :]
        q_in = (rq * jnp.exp2(b)).astype(BF16)
        k_out = (rk * jnp.exp2(b_last - b)).astype(BF16)
        o = _dot_nt(q_in, st.astype(BF16)) + jnp.concatenate(
            [_dot(quads[0], rv[0:half_span, :]), _dot(s_hi, rv)], axis=0)
        st_ref[hd] = st * jnp.exp2(b_last) + _dot_tn(rv, k_out)
        o = _rms(o) * gn * gate
        mix_scr[rows, ATTN_WIDTH + hd * HGRN_DIM:ATTN_WIDTH + (hd + 1) * HGRN_DIM] = o.astype(BF16)

    attn_blocks = [(g, hk) for g in range(n_groups) for hk in range(ATTN_KV_HEADS)]
    hgrn_units = [(hd, sp) for sp in range(n_spans) for hd in range(HGRN_HEADS)]
    for i in range(max(len(attn_blocks), len(hgrn_units))):
        if i < len(attn_blocks):
            attn_block(*attn_blocks[i])
        if i < len(hgrn_units):
            hgrn_unit(*hgrn_units[i])

    for hk in range(ATTN_KV_HEADS):
        kbuf[hk, 0:GROUP_ROWS, :] = kbuf[hk, tm:tm + GROUP_ROWS, :]
        vbuf[hk, 0:GROUP_ROWS, :] = vbuf[hk, tm:tm + GROUP_ROWS, :]

    y = _dot(mix_scr[...], wout_ref[...])
    o_ref[0] = x_ref[0] + _rms(y) * (gt * npost_ref[...])


def _mixer_call(x, mod, npre, npost, w_in, w_out, cos_t, sin_t, sinks, lb_logits, gnorm, *, tm, layer):
    bsz, seq, d = x.shape
    d_proj = w_in.shape[1]
    d_mix = w_out.shape[0]
    assert tm % (BAND_CHUNKS * CHUNK) == 0 and tm % HGRN_SPAN == 0
    r = jnp.arange(HGRN_SPAN)
    tril = (r[None, :] <= r[:, None]).astype(BF16)
    xspec = pl.BlockSpec((1, tm, d), lambda b, s: (b, s, 0))
    tspec = pl.BlockSpec((1, tm, LANES), lambda b, s: (b, s, 0))
    return pl.pallas_call(
        functools.partial(_mixer_kernel, tm=tm, layer=layer),
        grid=(bsz, seq // tm),
        in_specs=[
            pl.BlockSpec(memory_space=pltpu.SMEM),
            xspec,
            pl.BlockSpec((1, N_MOD, d), lambda b, s: (b, 0, 0)),
            _resident((1, d)),
            _resident((1, d)),
            _resident((d, d_proj)),
            _resident((d_mix, d)),
            tspec,
            tspec,
            _resident(lb_logits.shape),
            _resident((1, HGRN_DIM)),
            _resident((HGRN_SPAN, HGRN_SPAN)),
        ],
        out_specs=xspec,
        out_shape=jax.ShapeDtypeStruct(x.shape, F32),
        scratch_shapes=[
            pltpu.VMEM((ATTN_KV_HEADS, GROUP_ROWS + tm, LANES), BF16),
            pltpu.VMEM((ATTN_KV_HEADS, GROUP_ROWS + tm, LANES), BF16),
            pltpu.VMEM((HGRN_HEADS, HGRN_DIM, HGRN_DIM), F32),
            pltpu.VMEM((tm, d_mix), BF16),
            pltpu.VMEM((tm, HGRN_WIDTH), F32),
        ],
        compiler_params=pltpu.CompilerParams(
            dimension_semantics=("arbitrary", "arbitrary"),
            vmem_limit_bytes=VMEM_LIMIT_BYTES),
        name="mixer",
    )(sinks, x, mod, npre, npost, w_in, w_out, cos_t, sin_t, lb_logits, gnorm, tril)


def kernel(x, c, positions, w_cond, b_cond, norm_pre, norm_post, ffn_w_in, ffn_w_out,
           w_mix_in, w_mix_out, attn_sinks, hgrn_lb_logits, hgrn_gnorm):
    bsz, seq, d = x.shape
    depth = w_cond.shape[0]
    tm_ffn = min(seq, 1024)
    tm_mix = min(seq, 512)

    inv_freq = 1.0 / (ROPE_THETA ** (jnp.arange(0, ATTN_HEAD_DIM, 2, dtype=F32) / ATTN_HEAD_DIM))
    inv_row = jnp.tile(inv_freq, LANES // ROPE_FREQS)[None, :]
    pos_rep = jnp.broadcast_to(positions.reshape(-1, 1), (bsz * seq, ROPE_FREQS)).reshape(-1, LANES)
    cos_t, sin_t = _rope_call(pos_rep, inv_row)
    cos_t = cos_t.reshape(bsz, seq, LANES)
    sin_t = sin_t.reshape(bsz, seq, LANES)

    w_ffn_in = ffn_w_in.astype(BF16)
    w_ffn_out = ffn_w_out.astype(BF16)
    for layer in range(depth):
        mod = _mod_call(c, w_cond[layer], b_cond[layer]).reshape(bsz, N_MOD, d)
        npre = norm_pre[layer][:, None, :]
        npost = norm_post[layer][:, None, :]
        x = _ffn_call(x, mod, npre[0], npost[0], w_ffn_in[layer], w_ffn_out[layer],
                      which=0, tm=tm_ffn)
        x = _mixer_call(x, mod, npre[1], npost[1], w_mix_in[layer].astype(BF16),
                        w_mix_out[layer].astype(BF16), cos_t, sin_t, attn_sinks[layer],
                        hgrn_lb_logits, hgrn_gnorm[layer][None, :], tm=tm_mix, layer=layer)
        x = _ffn_call(x, mod, npre[2], npost[2], w_ffn_in[layer], w_ffn_out[layer],
                      which=1, tm=tm_ffn)
    return x
```

```python
import functools

import jax
import jax.numpy as jnp
from jax import lax
from jax.experimental import pallas as pl
from jax.experimental.pallas import tpu as pltpu

F32 = jnp.float32
BF16 = jnp.bfloat16

EPS = 1e-6
NEG_INF = -1e30
ROPE_THETA = 10000.0
LOG2E = 1.4426950408889634

CHUNK = 64
ATTN_HEAD_DIM = 64
ATTN_Q_HEADS = 8
ATTN_KV_HEADS = 2
ATTN_GROUP = ATTN_Q_HEADS // ATTN_KV_HEADS
ATTN_WIDTH = ATTN_Q_HEADS * ATTN_HEAD_DIM
KV_WIDTH = ATTN_KV_HEADS * ATTN_HEAD_DIM
HGRN_HEADS = 4
HGRN_DIM = 128
HGRN_WIDTH = HGRN_HEADS * HGRN_DIM
N_MOD = 9
FFN_RES_WEIGHT = 0.5

OFF_AQ = 0
OFF_AKV = OFF_AQ + ATTN_WIDTH
OFF_HQ = OFF_AKV + 2 * KV_WIDTH
OFF_HF = OFF_HQ + HGRN_WIDTH
OFF_HI = OFF_HF + HGRN_WIDTH
OFF_HG = OFF_HI + HGRN_WIDTH

LANES = 128
SUBLANES = 8
GROUP_ROWS = 2 * CHUNK
BAND_CHUNKS = 4
BAND_ROWS = BAND_CHUNKS * CHUNK
FFN_COL_TILE = 256
FFN_EDGE_ROWS = 512
MOD_COL_TILE = 2304
ROPE_FREQS = ATTN_HEAD_DIM // 2
HGRN_SPAN = 256
VMEM_LIMIT_BYTES = 56 * 1024 * 1024

FLAG_LANE = ATTN_HEAD_DIM + BAND_CHUNKS

DIRECT_SPAN = 4


def _sigmoid(v):
    return 1.0 / (1.0 + jnp.exp(-v))


def _silu(v):
    hv = 0.5 * v
    return hv + hv * jnp.tanh(hv)


def _rms(v):
    return v * lax.rsqrt(jnp.mean(v * v, axis=-1, keepdims=True) + EPS)


def _modulated(x, gain, shift):
    return _rms(x).astype(BF16) * gain.astype(BF16) + shift.astype(BF16)


def _dot(a, b):
    return jnp.dot(a, b, preferred_element_type=F32)


def _dot_nt(a, b):
    return lax.dot_general(a, b, (((1,), (1,)), ((), ())), preferred_element_type=F32)


def _dot_tn(a, b):
    return lax.dot_general(a, b, (((0,), (0,)), ((), ())), preferred_element_type=F32)


def _mod_kernel(c_ref, w_ref, b_ref, o_ref):
    c = c_ref[...]
    ca = (c * _sigmoid(c)).astype(BF16)
    o_ref[...] = _dot(ca, w_ref[...].astype(BF16)) + b_ref[...]


def _mod_call(c, w, b):
    bsz, d = c.shape
    n = w.shape[1]
    tn = MOD_COL_TILE
    return pl.pallas_call(
        _mod_kernel,
        grid=(n // tn,),
        in_specs=[
            pl.BlockSpec((bsz, d), lambda j: (0, 0)),
            pl.BlockSpec((d, tn), lambda j: (0, j)),
            pl.BlockSpec((1, tn), lambda j: (0, j)),
        ],
        out_specs=pl.BlockSpec((bsz, tn), lambda j: (0, j)),
        out_shape=jax.ShapeDtypeStruct((bsz, n), F32),
        compiler_params=pltpu.CompilerParams(dimension_semantics=("arbitrary",)),
        name="mod",
    )(c, w, b.reshape(1, n))


def _rope_kernel(pos_ref, inv_ref, cos_ref, sin_ref):
    tr = pos_ref.shape[0]
    ang = pos_ref[...].astype(F32) * inv_ref[...]
    group = lax.broadcasted_iota(jnp.int32, (tr, LANES), 1) // ROPE_FREQS
    sign = jnp.where(group % 2 == 0, -1.0, 1.0)
    n_grp = LANES // ROPE_FREQS
    for tab, out_ref, signed in ((jnp.cos(ang), cos_ref, False), (jnp.sin(ang), sin_ref, True)):
        rolled = [tab] + [pltpu.roll(tab, ROPE_FREQS * k, 1) for k in range(1, n_grp)]
        for a in range(n_grp):
            piece = rolled[(0 - a) % n_grp]
            for kk in range(1, n_grp):
                piece = jnp.where(group == kk, rolled[(kk - a) % n_grp], piece)
            if signed:
                piece = piece * sign
            out_ref[pl.ds(a, tr, stride=n_grp), :] = piece


def _rope_call(pos_rep, inv_row):
    rows = pos_rep.shape[0]
    n_grp = LANES // ROPE_FREQS
    tr = min(rows, 512)
    out_spec = pl.BlockSpec((n_grp * tr, LANES), lambda i: (i, 0))
    return pl.pallas_call(
        _rope_kernel,
        grid=(rows // tr,),
        in_specs=[pl.BlockSpec((tr, LANES), lambda i: (i, 0)),
                  pl.BlockSpec((1, LANES), lambda i: (0, 0))],
        out_specs=[out_spec, out_spec],
        out_shape=[jax.ShapeDtypeStruct((n_grp * rows, LANES), F32)] * 2,
        compiler_params=pltpu.CompilerParams(dimension_semantics=("arbitrary",)),
        name="rope",
    )(pos_rep, inv_row)


def _resident(shape, index=None):
    index = (0,) * len(shape) if index is None else index
    return pl.BlockSpec(shape, lambda b, s: index, pipeline_mode=pl.Buffered(1))


def _ffn_kernel(x_ref, mod_ref, npre_ref, npost_ref, win_ref, wout_ref, o_ref, *, sub, d_ff):
    sh = mod_ref[0, 3 * sub:3 * sub + 1, :]
    sc = mod_ref[0, 3 * sub + 1:3 * sub + 2, :]
    gt = mod_ref[0, 3 * sub + 2:3 * sub + 3, :]
    tm = x_ref.shape[1]
    n_chunks = d_ff // FFN_COL_TILE
    gain = npre_ref[...] * (1.0 + sc)
    out_gain = (FFN_RES_WEIGHT * gt) * npost_ref[...]

    def chunk(h_rows, j):
        lo = j * FFN_COL_TILE
        g = _dot(h_rows, win_ref[0, :, lo:lo + FFN_COL_TILE])
        u = _dot(h_rows, win_ref[0, :, d_ff + lo:d_ff + lo + FFN_COL_TILE])
        a = (_silu(g) * u).astype(BF16)
        return _dot(a, wout_ref[0, lo:lo + FFN_COL_TILE, :])

    slabs = [slice(r, r + FFN_EDGE_ROWS) for r in range(0, tm, FFN_EDGE_ROWS)]
    h_parts, acc_parts = [], []
    for rows in slabs:
        h_rows = _modulated(x_ref[0, rows, :], gain, sh)
        h_parts.append(h_rows)
        acc_parts.append(chunk(h_rows, 0))
    h = jnp.concatenate(h_parts, axis=0)
    acc = jnp.concatenate(acc_parts, axis=0)
    for j in range(1, n_chunks - 1):
        acc = acc + chunk(h, j)
    for rows in slabs:
        acc_rows = acc[rows, :] + chunk(h[rows, :], n_chunks - 1)
        o_ref[0, rows, :] = x_ref[0, rows, :] + _rms(acc_rows) * out_gain


def _ffn_call(x, mod, npre, npost, w_in, w_out, *, which, tm):
    bsz, seq, d = x.shape
    d_ff = w_out.shape[1]
    sub = 2 * which
    xspec = pl.BlockSpec((1, tm, d), lambda b, s: (b, s, 0))
    return pl.pallas_call(
        functools.partial(_ffn_kernel, sub=sub, d_ff=d_ff),
        grid=(bsz, seq // tm),
        in_specs=[
            xspec,
            pl.BlockSpec((1, N_MOD, d), lambda b, s: (b, 0, 0)),
            _resident((1, d)),
            _resident((1, d)),
            _resident((1, d, 2 * d_ff), (which, 0, 0)),
            _resident((1, d_ff, d), (which, 0, 0)),
        ],
        out_specs=xspec,
        out_shape=jax.ShapeDtypeStruct(x.shape, F32),
        compiler_params=pltpu.CompilerParams(
            dimension_semantics=("arbitrary", "arbitrary"),
            vmem_limit_bytes=VMEM_LIMIT_BYTES),
        name="ffn%d" % sub,
    )(x, mod, npre, npost, w_in, w_out)


def _mixer_kernel(sinks_ref, x_ref, mod_ref, npre_ref, npost_ref, win_ref, wout_ref,
                  cos_ref, sin_ref, lbl_ref, gn_ref, tril_ref, o_ref,
                  kbuf, vbuf, st_ref, mix_scr, *, tm, layer):
    si = pl.program_id(1)
    n_groups = tm // GROUP_ROWS
    n_spans = tm // HGRN_SPAN

    lane = lax.broadcasted_iota(jnp.int32, (tm, LANES), 1)
    row = lax.broadcasted_iota(jnp.int32, (tm, LANES), 0)
    low_head = lane < ATTN_HEAD_DIM

    @pl.when(si == 0)
    def _():
        carry_lane = lax.broadcasted_iota(jnp.int32, (GROUP_ROWS, LANES), 1)
        k0 = jnp.where(carry_lane == FLAG_LANE, 1.0, 0.0).astype(BF16)
        v0 = jnp.where(carry_lane < ATTN_HEAD_DIM, 0.0, 1.0).astype(BF16)
        for hk in range(ATTN_KV_HEADS):
            kbuf[hk, 0:GROUP_ROWS, :] = k0
            vbuf[hk, 0:GROUP_ROWS, :] = v0
        st_ref[...] = jnp.zeros(st_ref.shape, F32)

    x = x_ref[0]
    sh = mod_ref[0, 3:4, :]
    sc = mod_ref[0, 4:5, :]
    gt = mod_ref[0, 5:6, :]
    h = _modulated(x, npre_ref[...] * (1.0 + sc), sh)

    cos = cos_ref[0]
    sin = sin_ref[0]
    first_half = (lane % ATTN_HEAD_DIM) < (ATTN_HEAD_DIM // 2)
    chunk_id = (row // CHUNK) % BAND_CHUNKS
    key_extra = jnp.where(lane == ATTN_HEAD_DIM + chunk_id, 1.0, 0.0)
    hidden = ATTN_HEAD_DIM + (chunk_id + 1) % BAND_CHUNKS
    query_extra = jnp.where((lane == hidden) | (lane == FLAG_LANE), NEG_INF, 0.0)

    def rope(t):
        rot = jnp.where(first_half, pltpu.roll(t, LANES - 32, 1), pltpu.roll(t, 32, 1))
        return t * cos + rot * sin

    aq = _dot(h, win_ref[:, OFF_AQ:OFF_AQ + ATTN_WIDTH])
    akv = _dot(h, win_ref[:, OFF_AKV:OFF_AKV + 2 * KV_WIDTH])
    k = rope(akv[:, 0:KV_WIDTH])
    v = akv[:, KV_WIDTH:2 * KV_WIDTH]
    for src, buf, extra in ((k, kbuf, key_extra), (v, vbuf, 1.0)):
        buf[0, GROUP_ROWS:GROUP_ROWS + tm, :] = jnp.where(low_head, src, extra).astype(BF16)
        buf[1, GROUP_ROWS:GROUP_ROWS + tm, :] = jnp.where(
            low_head, pltpu.roll(src, ATTN_HEAD_DIM, 1), extra).astype(BF16)

    qh = []
    for jb in range(ATTN_Q_HEADS // 2):
        qb = rope(aq[:, jb * LANES:(jb + 1) * LANES]) * (ATTN_HEAD_DIM ** -0.5 * LOG2E)
        qh.append(jnp.where(low_head, qb, query_extra).astype(BF16))
        qh.append(jnp.where(low_head, pltpu.roll(qb, ATTN_HEAD_DIM, 1), query_extra).astype(BF16))

    low_grp = lax.broadcasted_iota(jnp.int32, (GROUP_ROWS, LANES), 1) < ATTN_HEAD_DIM

    def attn_block(g, hk):
        r0 = g * GROUP_ROWS
        heads = [hk * ATTN_GROUP + i for i in range(ATTN_GROUP)]
        qs = jnp.concatenate([qh[hd][r0:r0 + GROUP_ROWS, :] for hd in heads], axis=0)
        kband = kbuf[hk, r0:r0 + BAND_ROWS, :]
        vband = vbuf[hk, r0:r0 + BAND_ROWS, :]
        s = _dot_nt(qs, kband)
        m = jnp.max(s, axis=-1, keepdims=True)
        p = jnp.exp2(s - m)
        o = _dot(p.astype(BF16), vband)
        for pair in range(ATTN_GROUP // 2):
            res = []
            for odd in range(2):
                a0 = (2 * pair + odd) * GROUP_ROWS
                o_h = o[a0:a0 + GROUP_ROWS, :]
                sink_term = jnp.exp2(sinks_ref[heads[2 * pair + odd]] * LOG2E
                                     - m[a0:a0 + GROUP_ROWS, :])
                swapped = pltpu.roll(o_h, ATTN_HEAD_DIM, 1)
                if odd:
                    res.append(swapped / (o_h + sink_term))
                else:
                    res.append(o_h / (swapped + sink_term))
            col = (hk * (ATTN_GROUP // 2) + pair) * LANES
            mix_scr[r0:r0 + GROUP_ROWS, col:col + LANES] = (
                jnp.where(low_grp, res[0], res[1]).astype(BF16))

    lbl = lbl_ref[...]
    e_lb = jnp.exp(lbl - jnp.max(lbl, axis=0, keepdims=True))
    lb_all = jnp.sum(e_lb[0:layer + 1, :], axis=0, keepdims=True) / jnp.sum(e_lb, axis=0, keepdims=True)

    hq = _dot(h, win_ref[:, OFF_HQ:OFF_HQ + HGRN_WIDTH])
    hf = _dot(h, win_ref[:, OFF_HF:OFF_HF + HGRN_WIDTH])
    hi = _dot(h, win_ref[:, OFF_HI:OFF_HI + HGRN_WIDTH])
    hg = _dot(h, win_ref[:, OFF_HG:OFF_HG + HGRN_WIDTH])

    f = lb_all + (1.0 - lb_all) * _sigmoid(hf)
    one_minus_f = 1.0 - f
    g = jnp.log2(f)
    g1 = g.astype(BF16)
    g2 = (g - g1.astype(F32)).astype(BF16)
    tril = tril_ref[...]
    b_spans = []
    for sp in range(n_spans):
        rows = slice(sp * HGRN_SPAN, (sp + 1) * HGRN_SPAN)
        b_spans.append(_dot(tril, g1[rows, :]) + _dot(tril, g2[rows, :]))

    half_span = HGRN_SPAN // 2
    level_halves = []
    width = half_span
    while width >= DIRECT_SPAN:
        level_halves.append(width)
        width //= 2
    ti = lax.broadcasted_iota(jnp.int32, (half_span, half_span), 0)
    sj = lax.broadcasted_iota(jnp.int32, (half_span, half_span), 1)
    quad_mask = {}
    for half in level_halves[1:]:
        blk = 2 * half
        quad_mask[half] = ((ti // blk) == (sj // blk)) & ((ti % blk) >= half) & ((sj % blk) < half)
    diag = [sj == ti - delta for delta in range(DIRECT_SPAN)]
    sub_row = lax.broadcasted_iota(jnp.int32, (HGRN_SPAN, LANES), 0) % SUBLANES
    later4 = (sub_row % (2 * DIRECT_SPAN)) >= DIRECT_SPAN
    in_vreg = lax.broadcasted_iota(jnp.int32, (1, SUBLANES, LANES), 1)
    direct_ok = [(in_vreg % DIRECT_SPAN) >= delta for delta in range(DIRECT_SPAN)]

    gn = gn_ref[...]

    def hgrn_unit(hd, sp):
        cols = slice(hd * HGRN_DIM, (hd + 1) * HGRN_DIM)
        base = sp * HGRN_SPAN
        rows = slice(base, base + HGRN_SPAN)
        rq = _silu(hq[rows, cols]) * (HGRN_DIM ** -0.5)
        rk = one_minus_f[rows, cols]
        rv = hi[rows, cols].astype(BF16)
        gate = _silu(hg[rows, cols])
        b = b_spans[sp][:, cols]

        lvl_w = {}
        for half in level_halves:
            blk = 2 * half
            expo, operand = [], []
            for mblk in range(HGRN_SPAN // blk):
                lo = mblk * blk
                ref_row = b[lo + half - 1:lo + half, :]
                if half >= SUBLANES:
                    ref_half = jnp.broadcast_to(ref_row, (half, HGRN_DIM))
                    expo += [ref_half - b[lo:lo + half, :], b[lo + half:lo + blk, :] - ref_half]
                    operand += [rk[lo:lo + half, :], rq[lo + half:lo + blk, :]]
                else:
                    expo.append(jnp.broadcast_to(ref_row, (blk, HGRN_DIM)))
            if half >= SUBLANES:
                w = jnp.concatenate(operand, axis=0) * jnp.exp2(jnp.concatenate(expo, axis=0))
            else:
                dist = jnp.abs(b - jnp.concatenate(expo, axis=0))
                w = jnp.where(later4, rq, rk) * jnp.exp2(-dist)
            lvl_w[half] = w.astype(BF16)

        def grouped(t):
            return t.reshape(HGRN_SPAN // SUBLANES, SUBLANES, HGRN_DIM)

        b3, rq3, rk3 = grouped(b), grouped(rq), grouped(rk)
        direct = []
        for delta in range(DIRECT_SPAN):
            if delta == 0:
                prod = rq * rk
            else:
                gap = jnp.where(direct_ok[delta], b3 - pltpu.roll(b3, delta, 1), NEG_INF)
                prod = (rq3 * pltpu.roll(rk3, delta, 1) * jnp.exp2(gap)).reshape(HGRN_SPAN, HGRN_DIM)
            direct.append(jnp.sum(prod, axis=-1, keepdims=True))

        top = _dot_nt(lvl_w[half_span][half_span:, :], lvl_w[half_span][0:half_span, :])
        quads = []
        for qd in range(2):
            o0 = qd * half_span
            sc_q = jnp.zeros((half_span, half_span), F32)
            for delta in range(DIRECT_SPAN):
                sc_q = jnp.where(diag[delta], direct[delta][o0:o0 + half_span, :], sc_q)
            for half in level_halves[1:]:
                w_q = lvl_w[half][o0:o0 + half_span, :]
                sc_q = jnp.where(quad_mask[half], _dot_nt(w_q, w_q), sc_q)
            quads.append(sc_q.astype(BF16))
        s_hi = jnp.concatenate([top.astype(BF16), quads[1]], axis=1)

        st = st_ref[hd]
        b_last = b[HGRN_SPAN - 1:HGRN_SPAN, :]
        q_in = (rq * jnp.exp2(b)).astype(BF16)
        k_out = (rk * jnp.exp2(b_last - b)).astype(BF16)
        o = _dot_nt(q_in, st.astype(BF16)) + jnp.concatenate(
            [_dot(quads[0], rv[0:half_span, :]), _dot(s_hi, rv)], axis=0)
        st_ref[hd] = st * jnp.exp2(b_last) + _dot_tn(rv, k_out)
        o = _rms(o) * gn * gate
        mix_scr[rows, ATTN_WIDTH + hd * HGRN_DIM:ATTN_WIDTH + (hd + 1) * HGRN_DIM] = o.astype(BF16)

    attn_blocks = [(g, hk) for g in range(n_groups) for hk in range(ATTN_KV_HEADS)]
    hgrn_units = [(hd, sp) for sp in range(n_spans) for hd in range(HGRN_HEADS)]
    for i in range(max(len(attn_blocks), len(hgrn_units))):
        if i < len(attn_blocks):
            attn_block(*attn_blocks[i])
        if i < len(hgrn_units):
            hgrn_unit(*hgrn_units[i])

    for hk in range(ATTN_KV_HEADS):
        kbuf[hk, 0:GROUP_ROWS, :] = kbuf[hk, tm:tm + GROUP_ROWS, :]
        vbuf[hk, 0:GROUP_ROWS, :] = vbuf[hk, tm:tm + GROUP_ROWS, :]

    y = _dot(mix_scr[...], wout_ref[...])
    o_ref[0] = x_ref[0] + _rms(y) * (gt * npost_ref[...])


def _mixer_call(x, mod, npre, npost, w_in, w_out, cos_t, sin_t, sinks, lb_logits, gnorm, *, tm, layer):
    bsz, seq, d = x.shape
    d_proj = w_in.shape[1]
    d_mix = w_out.shape[0]
    assert tm % (BAND_CHUNKS * CHUNK) == 0 and tm % HGRN_SPAN == 0
    r = jnp.arange(HGRN_SPAN)
    tril = (r[None, :] <= r[:, None]).astype(BF16)
    xspec = pl.BlockSpec((1, tm, d), lambda b, s: (b, s, 0))
    tspec = pl.BlockSpec((1, tm, LANES), lambda b, s: (b, s, 0))
    return pl.pallas_call(
        functools.partial(_mixer_kernel, tm=tm, layer=layer),
        grid=(bsz, seq // tm),
        in_specs=[
            pl.BlockSpec(memory_space=pltpu.SMEM),
            xspec,
            pl.BlockSpec((1, N_MOD, d), lambda b, s: (b, 0, 0)),
            _resident((1, d)),
            _resident((1, d)),
            _resident((d, d_proj)),
            _resident((d_mix, d)),
            tspec,
            tspec,
            _resident(lb_logits.shape),
            _resident((1, HGRN_DIM)),
            _resident((HGRN_SPAN, HGRN_SPAN)),
        ],
        out_specs=xspec,
        out_shape=jax.ShapeDtypeStruct(x.shape, F32),
        scratch_shapes=[
            pltpu.VMEM((ATTN_KV_HEADS, GROUP_ROWS + tm, LANES), BF16),
            pltpu.VMEM((ATTN_KV_HEADS, GROUP_ROWS + tm, LANES), BF16),
            pltpu.VMEM((HGRN_HEADS, HGRN_DIM, HGRN_DIM), F32),
            pltpu.VMEM((tm, d_mix), BF16),
        ],
        compiler_params=pltpu.CompilerParams(
            dimension_semantics=("arbitrary", "arbitrary"),
            vmem_limit_bytes=VMEM_LIMIT_BYTES),
        name="mixer",
    )(sinks, x, mod, npre, npost, w_in, w_out, cos_t, sin_t, lb_logits, gnorm, tril)


def kernel(x, c, positions, w_cond, b_cond, norm_pre, norm_post, ffn_w_in, ffn_w_out,
           w_mix_in, w_mix_out, attn_sinks, hgrn_lb_logits, hgrn_gnorm):
    bsz, seq, d = x.shape
    depth = w_cond.shape[0]
    tm_ffn = min(seq, 1024)
    tm_mix = min(seq, 512)

    inv_freq = 1.0 / (ROPE_THETA ** (jnp.arange(0, ATTN_HEAD_DIM, 2, dtype=F32) / ATTN_HEAD_DIM))
    inv_row = jnp.tile(inv_freq, LANES // ROPE_FREQS)[None, :]
    pos_rep = jnp.broadcast_to(positions.reshape(-1, 1), (bsz * seq, ROPE_FREQS)).reshape(-1, LANES)
    cos_t, sin_t = _rope_call(pos_rep, inv_row)
    cos_t = cos_t.reshape(bsz, seq, LANES)
    sin_t = sin_t.reshape(bsz, seq, LANES)

    w_ffn_in = ffn_w_in.astype(BF16)
    w_ffn_out = ffn_w_out.astype(BF16)
    for layer in range(depth):
        mod = _mod_call(c, w_cond[layer], b_cond[layer]).reshape(bsz, N_MOD, d)
        npre = norm_pre[layer][:, None, :]
        npost = norm_post[layer][:, None, :]
        x = _ffn_call(x, mod, npre[0], npost[0], w_ffn_in[layer], w_ffn_out[layer],
                      which=0, tm=tm_ffn)
        x = _mixer_call(x, mod, npre[1], npost[1], w_mix_in[layer].astype(BF16),
                        w_mix_out[layer].astype(BF16), cos_t, sin_t, attn_sinks[layer],
                        hgrn_lb_logits, hgrn_gnorm[layer][None, :], tm=tm_mix, layer=layer)
        x = _ffn_call(x, mod, npre[2], npost[2], w_ffn_in[layer], w_ffn_out[layer],
                      which=1, tm=tm_ffn)
    return x
```

```python
import functools

import jax
import jax.numpy as jnp
from jax import lax
from jax.experimental import pallas as pl
from jax.experimental.pallas import tpu as pltpu

F32 = jnp.float32
BF16 = jnp.bfloat16

EPS = 1e-6
NEG_INF = -1e30
ROPE_THETA = 10000.0
LOG2E = 1.4426950408889634

CHUNK = 64
ATTN_HEAD_DIM = 64
ATTN_Q_HEADS = 8
ATTN_KV_HEADS = 2
ATTN_GROUP = ATTN_Q_HEADS // ATTN_KV_HEADS
ATTN_WIDTH = ATTN_Q_HEADS * ATTN_HEAD_DIM
KV_WIDTH = ATTN_KV_HEADS * ATTN_HEAD_DIM
HGRN_HEADS = 4
HGRN_DIM = 128
HGRN_WIDTH = HGRN_HEADS * HGRN_DIM
N_MOD = 9
FFN_RES_WEIGHT = 0.5

OFF_AQ = 0
OFF_AKV = OFF_AQ + ATTN_WIDTH
OFF_HQ = OFF_AKV + 2 * KV_WIDTH
OFF_HF = OFF_HQ + HGRN_WIDTH
OFF_HI = OFF_HF + HGRN_WIDTH
OFF_HG = OFF_HI + HGRN_WIDTH

LANES = 128
SUBLANES = 8
GROUP_ROWS = 2 * CHUNK
BAND_CHUNKS = 4
BAND_ROWS = BAND_CHUNKS * CHUNK
FFN_COL_TILE = 256
FFN_EDGE_ROWS = 512
MOD_COL_TILE = 2304
ROPE_FREQS = ATTN_HEAD_DIM // 2
HGRN_SPAN = 256
VMEM_LIMIT_BYTES = 56 * 1024 * 1024

FLAG_LANE = ATTN_HEAD_DIM + BAND_CHUNKS

DIRECT_SPAN = 4


def _sigmoid(v):
    return 1.0 / (1.0 + jnp.exp(-v))


def _silu(v):
    hv = 0.5 * v
    return hv + hv * jnp.tanh(hv)


def _rms(v):
    return v * lax.rsqrt(jnp.mean(v * v, axis=-1, keepdims=True) + EPS)


def _modulated(x, gain, shift):
    return _rms(x).astype(BF16) * gain.astype(BF16) + shift.astype(BF16)


def _dot(a, b):
    return jnp.dot(a, b, preferred_element_type=F32)


def _dot_nt(a, b):
    return lax.dot_general(a, b, (((1,), (1,)), ((), ())), preferred_element_type=F32)


def _dot_tn(a, b):
    return lax.dot_general(a, b, (((0,), (0,)), ((), ())), preferred_element_type=F32)


def _mod_kernel(c_ref, w_ref, b_ref, o_ref):
    c = c_ref[...]
    ca = (c * _sigmoid(c)).astype(BF16)
    o_ref[...] = _dot(ca, w_ref[...].astype(BF16)) + b_ref[...]


def _mod_call(c, w, b):
    bsz, d = c.shape
    n = w.shape[1]
    tn = MOD_COL_TILE
    return pl.pallas_call(
        _mod_kernel,
        grid=(n // tn,),
        in_specs=[
            pl.BlockSpec((bsz, d), lambda j: (0, 0)),
            pl.BlockSpec((d, tn), lambda j: (0, j)),
            pl.BlockSpec((1, tn), lambda j: (0, j)),
        ],
        out_specs=pl.BlockSpec((bsz, tn), lambda j: (0, j)),
        out_shape=jax.ShapeDtypeStruct((bsz, n), F32),
        compiler_params=pltpu.CompilerParams(dimension_semantics=("arbitrary",)),
        name="mod",
    )(c, w, b.reshape(1, n))


def _rope_kernel(pos_ref, inv_ref, cos_ref, sin_ref):
    tr = pos_ref.shape[0]
    ang = pos_ref[...].astype(F32) * inv_ref[...]
    group = lax.broadcasted_iota(jnp.int32, (tr, LANES), 1) // ROPE_FREQS
    sign = jnp.where(group % 2 == 0, -1.0, 1.0)
    n_grp = LANES // ROPE_FREQS
    for tab, out_ref, signed in ((jnp.cos(ang), cos_ref, False), (jnp.sin(ang), sin_ref, True)):
        rolled = [tab] + [pltpu.roll(tab, ROPE_FREQS * k, 1) for k in range(1, n_grp)]
        for a in range(n_grp):
            piece = rolled[(0 - a) % n_grp]
            for kk in range(1, n_grp):
                piece = jnp.where(group == kk, rolled[(kk - a) % n_grp], piece)
            if signed:
                piece = piece * sign
            out_ref[pl.ds(a, tr, stride=n_grp), :] = piece


def _rope_call(pos_rep, inv_row):
    rows = pos_rep.shape[0]
    n_grp = LANES // ROPE_FREQS
    tr = min(rows, 512)
    out_spec = pl.BlockSpec((n_grp * tr, LANES), lambda i: (i, 0))
    return pl.pallas_call(
        _rope_kernel,
        grid=(rows // tr,),
        in_specs=[pl.BlockSpec((tr, LANES), lambda i: (i, 0)),
                  pl.BlockSpec((1, LANES), lambda i: (0, 0))],
        out_specs=[out_spec, out_spec],
        out_shape=[jax.ShapeDtypeStruct((n_grp * rows, LANES), F32)] * 2,
        compiler_params=pltpu.CompilerParams(dimension_semantics=("arbitrary",)),
        name="rope",
    )(pos_rep, inv_row)


def _resident(shape, index=None):
    index = (0,) * len(shape) if index is None else index
    return pl.BlockSpec(shape, lambda b, s: index, pipeline_mode=pl.Buffered(1))


def _ffn_kernel(x_ref, mod_ref, npre_ref, npost_ref, win_ref, wout_ref, o_ref, *, sub, d_ff):
    sh = mod_ref[0, 3 * sub:3 * sub + 1, :]
    sc = mod_ref[0, 3 * sub + 1:3 * sub + 2, :]
    gt = mod_ref[0, 3 * sub + 2:3 * sub + 3, :]
    tm = x_ref.shape[1]
    n_chunks = d_ff // FFN_COL_TILE
    gain = npre_ref[...] * (1.0 + sc)
    out_gain = (FFN_RES_WEIGHT * gt) * npost_ref[...]

    def chunk(h_rows, j):
        lo = j * FFN_COL_TILE
        g = _dot(h_rows, win_ref[0, :, lo:lo + FFN_COL_TILE])
        u = _dot(h_rows, win_ref[0, :, d_ff + lo:d_ff + lo + FFN_COL_TILE])
        a = (_silu(g) * u).astype(BF16)
        return _dot(a, wout_ref[0, lo:lo + FFN_COL_TILE, :])

    slabs = [slice(r, r + FFN_EDGE_ROWS) for r in range(0, tm, FFN_EDGE_ROWS)]
    h_parts, acc_parts = [], []
    for rows in slabs:
        h_rows = _modulated(x_ref[0, rows, :], gain, sh)
        h_parts.append(h_rows)
        acc_parts.append(chunk(h_rows, 0))
    h = jnp.concatenate(h_parts, axis=0)
    acc = jnp.concatenate(acc_parts, axis=0)
    for j in range(1, n_chunks - 1):
        acc = acc + chunk(h, j)
    for rows in slabs:
        acc_rows = acc[rows, :] + chunk(h[rows, :], n_chunks - 1)
        o_ref[0, rows, :] = x_ref[0, rows, :] + _rms(acc_rows) * out_gain


def _ffn_call(x, mod, npre, npost, w_in, w_out, *, which, tm):
    bsz, seq, d = x.shape
    d_ff = w_out.shape[1]
    sub = 2 * which
    xspec = pl.BlockSpec((1, tm, d), lambda b, s: (b, s, 0))
    return pl.pallas_call(
        functools.partial(_ffn_kernel, sub=sub, d_ff=d_ff),
        grid=(bsz, seq // tm),
        in_specs=[
            xspec,
            pl.BlockSpec((1, N_MOD, d), lambda b, s: (b, 0, 0)),
            _resident((1, d)),
            _resident((1, d)),
            _resident((1, d, 2 * d_ff), (which, 0, 0)),
            _resident((1, d_ff, d), (which, 0, 0)),
        ],
        out_specs=xspec,
        out_shape=jax.ShapeDtypeStruct(x.shape, F32),
        compiler_params=pltpu.CompilerParams(
            dimension_semantics=("arbitrary", "arbitrary"),
            vmem_limit_bytes=VMEM_LIMIT_BYTES),
        name="ffn%d" % sub,
    )(x, mod, npre, npost, w_in, w_out)


def _mixer_kernel(sinks_ref, x_ref, mod_ref, npre_ref, npost_ref, win_ref, wout_ref,
                  cos_ref, sin_ref, lbl_ref, gn_ref, tril_ref, o_ref,
                  kbuf, vbuf, st_ref, mix_scr, *, tm, layer):
    si = pl.program_id(1)
    n_groups = tm // GROUP_ROWS
    n_spans = tm // HGRN_SPAN

    lane = lax.broadcasted_iota(jnp.int32, (tm, LANES), 1)
    row = lax.broadcasted_iota(jnp.int32, (tm, LANES), 0)
    low_head = lane < ATTN_HEAD_DIM

    @pl.when(si == 0)
    def _():
        carry_lane = lax.broadcasted_iota(jnp.int32, (GROUP_ROWS, LANES), 1)
        k0 = jnp.where(carry_lane == FLAG_LANE, 1.0, 0.0).astype(BF16)
        v0 = jnp.where(carry_lane < ATTN_HEAD_DIM, 0.0, 1.0).astype(BF16)
        for hk in range(ATTN_KV_HEADS):
            kbuf[hk] = k0
            vbuf[hk] = v0
        st_ref[...] = jnp.zeros(st_ref.shape, F32)

    x = x_ref[0]
    sh = mod_ref[0, 3:4, :]
    sc = mod_ref[0, 4:5, :]
    gt = mod_ref[0, 5:6, :]
    h = _modulated(x, npre_ref[...] * (1.0 + sc), sh)

    cos = cos_ref[0]
    sin = sin_ref[0]
    first_half = (lane % ATTN_HEAD_DIM) < (ATTN_HEAD_DIM // 2)
    chunk_id = (row // CHUNK) % BAND_CHUNKS
    key_extra = jnp.where(lane == ATTN_HEAD_DIM + chunk_id, 1.0, 0.0)
    hidden = ATTN_HEAD_DIM + (chunk_id + 1) % BAND_CHUNKS
    query_extra = jnp.where((lane == hidden) | (lane == FLAG_LANE), NEG_INF, 0.0)

    def rope(t):
        rot = jnp.where(first_half, pltpu.roll(t, LANES - 32, 1), pltpu.roll(t, 32, 1))
        return t * cos + rot * sin

    aq = _dot(h, win_ref[:, OFF_AQ:OFF_AQ + ATTN_WIDTH])
    akv = _dot(h, win_ref[:, OFF_AKV:OFF_AKV + 2 * KV_WIDTH])
    k = rope(akv[:, 0:KV_WIDTH])
    v = akv[:, KV_WIDTH:2 * KV_WIDTH]
    k_heads = [jnp.where(low_head, k, key_extra).astype(BF16),
               jnp.where(low_head, pltpu.roll(k, ATTN_HEAD_DIM, 1), key_extra).astype(BF16)]
    v_heads = [jnp.where(low_head, v, 1.0).astype(BF16),
               jnp.where(low_head, pltpu.roll(v, ATTN_HEAD_DIM, 1), 1.0).astype(BF16)]

    def band(heads, buf, g, hk):
        if g == 0:
            return jnp.concatenate([buf[hk], heads[hk][0:GROUP_ROWS, :]], axis=0)
        return heads[hk][(g - 1) * GROUP_ROWS:(g + 1) * GROUP_ROWS, :]

    qh = []
    for jb in range(ATTN_Q_HEADS // 2):
        qb = rope(aq[:, jb * LANES:(jb + 1) * LANES]) * (ATTN_HEAD_DIM ** -0.5 * LOG2E)
        qh.append(jnp.where(low_head, qb, query_extra).astype(BF16))
        qh.append(jnp.where(low_head, pltpu.roll(qb, ATTN_HEAD_DIM, 1), query_extra).astype(BF16))

    low_grp = lax.broadcasted_iota(jnp.int32, (GROUP_ROWS, LANES), 1) < ATTN_HEAD_DIM

    def attn_block(g, hk):
        r0 = g * GROUP_ROWS
        heads = [hk * ATTN_GROUP + i for i in range(ATTN_GROUP)]
        qs = jnp.concatenate([qh[hd][r0:r0 + GROUP_ROWS, :] for hd in heads], axis=0)
        kband = band(k_heads, kbuf, g, hk)
        vband = band(v_heads, vbuf, g, hk)
        s = _dot_nt(qs, kband)
        m = jnp.max(s, axis=-1, keepdims=True)
        p = jnp.exp2(s - m)
        o = _dot(p.astype(BF16), vband)
        for pair in range(ATTN_GROUP // 2):
            res = []
            for odd in range(2):
                a0 = (2 * pair + odd) * GROUP_ROWS
                o_h = o[a0:a0 + GROUP_ROWS, :]
                sink_term = jnp.exp2(sinks_ref[heads[2 * pair + odd]] * LOG2E
                                     - m[a0:a0 + GROUP_ROWS, :])
                swapped = pltpu.roll(o_h, ATTN_HEAD_DIM, 1)
                if odd:
                    res.append(swapped / (o_h + sink_term))
                else:
                    res.append(o_h / (swapped + sink_term))
            col = (hk * (ATTN_GROUP // 2) + pair) * LANES
            mix_scr[r0:r0 + GROUP_ROWS, col:col + LANES] = (
                jnp.where(low_grp, res[0], res[1]).astype(BF16))

    lbl = lbl_ref[...]
    e_lb = jnp.exp(lbl - jnp.max(lbl, axis=0, keepdims=True))
    lb_all = jnp.sum(e_lb[0:layer + 1, :], axis=0, keepdims=True) / jnp.sum(e_lb, axis=0, keepdims=True)

    hq = _dot(h, win_ref[:, OFF_HQ:OFF_HQ + HGRN_WIDTH])
    hf = _dot(h, win_ref[:, OFF_HF:OFF_HF + HGRN_WIDTH])
    hi = _dot(h, win_ref[:, OFF_HI:OFF_HI + HGRN_WIDTH])
    hg = _dot(h, win_ref[:, OFF_HG:OFF_HG + HGRN_WIDTH])

    f = lb_all + (1.0 - lb_all) * _sigmoid(hf)
    one_minus_f = 1.0 - f
    g = jnp.log2(f)
    g1 = g.astype(BF16)
    g2 = (g - g1.astype(F32)).astype(BF16)
    tril = tril_ref[...]
    b_spans = []
    for sp in range(n_spans):
        rows = slice(sp * HGRN_SPAN, (sp + 1) * HGRN_SPAN)
        b_spans.append(_dot(tril, g1[rows, :]) + _dot(tril, g2[rows, :]))

    half_span = HGRN_SPAN // 2
    level_halves = []
    width = half_span
    while width >= DIRECT_SPAN:
        level_halves.append(width)
        width //= 2
    ti = lax.broadcasted_iota(jnp.int32, (half_span, half_span), 0)
    sj = lax.broadcasted_iota(jnp.int32, (half_span, half_span), 1)
    quad_mask = {}
    for half in level_halves[1:]:
        blk = 2 * half
        quad_mask[half] = ((ti // blk) == (sj // blk)) & ((ti % blk) >= half) & ((sj % blk) < half)
    diag = [sj == ti - delta for delta in range(DIRECT_SPAN)]
    sub_row = lax.broadcasted_iota(jnp.int32, (HGRN_SPAN, LANES), 0) % SUBLANES
    later4 = (sub_row % (2 * DIRECT_SPAN)) >= DIRECT_SPAN
    in_vreg = lax.broadcasted_iota(jnp.int32, (1, SUBLANES, LANES), 1)
    direct_ok = [(in_vreg % DIRECT_SPAN) >= delta for delta in range(DIRECT_SPAN)]

    gn = gn_ref[...]

    def hgrn_unit(hd, sp):
        cols = slice(hd * HGRN_DIM, (hd + 1) * HGRN_DIM)
        base = sp * HGRN_SPAN
        rows = slice(base, base + HGRN_SPAN)
        rq = _silu(hq[rows, cols]) * (HGRN_DIM ** -0.5)
        rk = one_minus_f[rows, cols]
        rv = hi[rows, cols].astype(BF16)
        gate = _silu(hg[rows, cols])
        b = b_spans[sp][:, cols]

        lvl_w = {}
        for half in level_halves:
            blk = 2 * half
            expo, operand = [], []
            for mblk in range(HGRN_SPAN // blk):
                lo = mblk * blk
                ref_row = b[lo + half - 1:lo + half, :]
                if half >= SUBLANES:
                    ref_half = jnp.broadcast_to(ref_row, (half, HGRN_DIM))
                    expo += [ref_half - b[lo:lo + half, :], b[lo + half:lo + blk, :] - ref_half]
                    operand += [rk[lo:lo + half, :], rq[lo + half:lo + blk, :]]
                else:
                    expo.append(jnp.broadcast_to(ref_row, (blk, HGRN_DIM)))
            if half >= SUBLANES:
                w = jnp.concatenate(operand, axis=0) * jnp.exp2(jnp.concatenate(expo, axis=0))
            else:
                dist = jnp.abs(b - jnp.concatenate(expo, axis=0))
                w = jnp.where(later4, rq, rk) * jnp.exp2(-dist)
            lvl_w[half] = w.astype(BF16)

        def grouped(t):
            return t.reshape(HGRN_SPAN // SUBLANES, SUBLANES, HGRN_DIM)

        b3, rq3, rk3 = grouped(b), grouped(rq), grouped(rk)
        direct = []
        for delta in range(DIRECT_SPAN):
            if delta == 0:
                prod = rq * rk
            else:
                gap = jnp.where(direct_ok[delta], b3 - pltpu.roll(b3, delta, 1), NEG_INF)
                prod = (rq3 * pltpu.roll(rk3, delta, 1) * jnp.exp2(gap)).reshape(HGRN_SPAN, HGRN_DIM)
            direct.append(jnp.sum(prod, axis=-1, keepdims=True))

        top = _dot_nt(lvl_w[half_span][half_span:, :], lvl_w[half_span][0:half_span, :])
        quads = []
        for qd in range(2):
            o0 = qd * half_span
            sc_q = jnp.zeros((half_span, half_span), F32)
            for delta in range(DIRECT_SPAN):
                sc_q = jnp.where(diag[delta], direct[delta][o0:o0 + half_span, :], sc_q)
            for half in level_halves[1:]:
                w_q = lvl_w[half][o0:o0 + half_span, :]
                sc_q = jnp.where(quad_mask[half], _dot_nt(w_q, w_q), sc_q)
            quads.append(sc_q.astype(BF16))
        s_hi = jnp.concatenate([top.astype(BF16), quads[1]], axis=1)

        st = st_ref[hd]
        b_last = b[HGRN_SPAN - 1:HGRN_SPAN, :]
        q_in = (rq * jnp.exp2(b)).astype(BF16)
        k_out = (rk * jnp.exp2(b_last - b)).astype(BF16)
        o = _dot_nt(q_in, st.astype(BF16)) + jnp.concatenate(
            [_dot(quads[0], rv[0:half_span, :]), _dot(s_hi, rv)], axis=0)
        st_ref[hd] = st * jnp.exp2(b_last) + _dot_tn(rv, k_out)
        o = _rms(o) * gn * gate
        mix_scr[rows, ATTN_WIDTH + hd * HGRN_DIM:ATTN_WIDTH + (hd + 1) * HGRN_DIM] = o.astype(BF16)

    attn_blocks = [(g, hk) for g in range(n_groups) for hk in range(ATTN_KV_HEADS)]
    hgrn_units = [(hd, sp) for sp in range(n_spans) for hd in range(HGRN_HEADS)]
    for i in range(max(len(attn_blocks), len(hgrn_units))):
        if i < len(attn_blocks):
            attn_block(*attn_blocks[i])
        if i < len(hgrn_units):
            hgrn_unit(*hgrn_units[i])

    for hk in range(ATTN_KV_HEADS):
        kbuf[hk] = k_heads[hk][tm - GROUP_ROWS:tm, :]
        vbuf[hk] = v_heads[hk][tm - GROUP_ROWS:tm, :]

    y = _dot(mix_scr[...], wout_ref[...])
    o_ref[0] = x_ref[0] + _rms(y) * (gt * npost_ref[...])


def _mixer_call(x, mod, npre, npost, w_in, w_out, cos_t, sin_t, sinks, lb_logits, gnorm, *, tm, layer):
    bsz, seq, d = x.shape
    d_proj = w_in.shape[1]
    d_mix = w_out.shape[0]
    assert tm % (BAND_CHUNKS * CHUNK) == 0 and tm % HGRN_SPAN == 0
    r = jnp.arange(HGRN_SPAN)
    tril = (r[None, :] <= r[:, None]).astype(BF16)
    xspec = pl.BlockSpec((1, tm, d), lambda b, s: (b, s, 0))
    tspec = pl.BlockSpec((1, tm, LANES), lambda b, s: (b, s, 0))
    return pl.pallas_call(
        functools.partial(_mixer_kernel, tm=tm, layer=layer),
        grid=(bsz, seq // tm),
        in_specs=[
            pl.BlockSpec(memory_space=pltpu.SMEM),
            xspec,
            pl.BlockSpec((1, N_MOD, d), lambda b, s: (b, 0, 0)),
            _resident((1, d)),
            _resident((1, d)),
            _resident((d, d_proj)),
            _resident((d_mix, d)),
            tspec,
            tspec,
            _resident(lb_logits.shape),
            _resident((1, HGRN_DIM)),
            _resident((HGRN_SPAN, HGRN_SPAN)),
        ],
        out_specs=xspec,
        out_shape=jax.ShapeDtypeStruct(x.shape, F32),
        scratch_shapes=[
            pltpu.VMEM((ATTN_KV_HEADS, GROUP_ROWS, LANES), BF16),
            pltpu.VMEM((ATTN_KV_HEADS, GROUP_ROWS, LANES), BF16),
            pltpu.VMEM((HGRN_HEADS, HGRN_DIM, HGRN_DIM), F32),
            pltpu.VMEM((tm, d_mix), BF16),
        ],
        compiler_params=pltpu.CompilerParams(
            dimension_semantics=("arbitrary", "arbitrary"),
            vmem_limit_bytes=VMEM_LIMIT_BYTES),
        name="mixer",
    )(sinks, x, mod, npre, npost, w_in, w_out, cos_t, sin_t, lb_logits, gnorm, tril)


def kernel(x, c, positions, w_cond, b_cond, norm_pre, norm_post, ffn_w_in, ffn_w_out,
           w_mix_in, w_mix_out, attn_sinks, hgrn_lb_logits, hgrn_gnorm):
    bsz, seq, d = x.shape
    depth = w_cond.shape[0]
    tm_ffn = min(seq, 1024)
    tm_mix = min(seq, 512)

    inv_freq = 1.0 / (ROPE_THETA ** (jnp.arange(0, ATTN_HEAD_DIM, 2, dtype=F32) / ATTN_HEAD_DIM))
    inv_row = jnp.tile(inv_freq, LANES // ROPE_FREQS)[None, :]
    pos_rep = jnp.broadcast_to(positions.reshape(-1, 1), (bsz * seq, ROPE_FREQS)).reshape(-1, LANES)
    cos_t, sin_t = _rope_call(pos_rep, inv_row)
    cos_t = cos_t.reshape(bsz, seq, LANES)
    sin_t = sin_t.reshape(bsz, seq, LANES)

    w_ffn_in = ffn_w_in.astype(BF16)
    w_ffn_out = ffn_w_out.astype(BF16)
    for layer in range(depth):
        mod = _mod_call(c, w_cond[layer], b_cond[layer]).reshape(bsz, N_MOD, d)
        npre = norm_pre[layer][:, None, :]
        npost = norm_post[layer][:, None, :]
        x = _ffn_call(x, mod, npre[0], npost[0], w_ffn_in[layer], w_ffn_out[layer],
                      which=0, tm=tm_ffn)
        x = _mixer_call(x, mod, npre[1], npost[1], w_mix_in[layer].astype(BF16),
                        w_mix_out[layer].astype(BF16), cos_t, sin_t, attn_sinks[layer],
                        hgrn_lb_logits, hgrn_gnorm[layer][None, :], tm=tm_mix, layer=layer)
        x = _ffn_call(x, mod, npre[2], npost[2], w_ffn_in[layer], w_ffn_out[layer],
                      which=1, tm=tm_ffn)
    return x
```

```python
import functools

import jax
import jax.numpy as jnp
from jax import lax
from jax.experimental import pallas as pl
from jax.experimental.pallas import tpu as pltpu

F32 = jnp.float32
BF16 = jnp.bfloat16

EPS = 1e-6
NEG_INF = -1e30
ROPE_THETA = 10000.0
LOG2E = 1.4426950408889634

CHUNK = 64
ATTN_HEAD_DIM = 64
ATTN_Q_HEADS = 8
ATTN_KV_HEADS = 2
ATTN_GROUP = ATTN_Q_HEADS // ATTN_KV_HEADS
ATTN_WIDTH = ATTN_Q_HEADS * ATTN_HEAD_DIM
KV_WIDTH = ATTN_KV_HEADS * ATTN_HEAD_DIM
HGRN_HEADS = 4
HGRN_DIM = 128
HGRN_WIDTH = HGRN_HEADS * HGRN_DIM
N_MOD = 9
FFN_RES_WEIGHT = 0.5

OFF_AQ = 0
OFF_AKV = OFF_AQ + ATTN_WIDTH
OFF_HQ = OFF_AKV + 2 * KV_WIDTH
OFF_HF = OFF_HQ + HGRN_WIDTH
OFF_HI = OFF_HF + HGRN_WIDTH
OFF_HG = OFF_HI + HGRN_WIDTH

LANES = 128
SUBLANES = 8
GROUP_ROWS = 2 * CHUNK
BAND_CHUNKS = 4
BAND_ROWS = BAND_CHUNKS * CHUNK
FFN_COL_TILE = 256
FFN_EDGE_ROWS = 512
MOD_COL_TILE = 2304
ROPE_FREQS = ATTN_HEAD_DIM // 2
HGRN_SPAN = 256
VMEM_LIMIT_BYTES = 56 * 1024 * 1024

FLAG_LANE = ATTN_HEAD_DIM + BAND_CHUNKS

DIRECT_SPAN = 4


def _sigmoid(v):
    return 1.0 / (1.0 + jnp.exp(-v))


def _silu(v):
    hv = 0.5 * v
    return hv + hv * jnp.tanh(hv)


def _rms(v):
    return v * lax.rsqrt(jnp.mean(v * v, axis=-1, keepdims=True) + EPS)


def _modulated(x, gain, shift):
    return _rms(x).astype(BF16) * gain.astype(BF16) + shift.astype(BF16)


def _dot(a, b):
    return jnp.dot(a, b, preferred_element_type=F32)


def _dot_nt(a, b):
    return lax.dot_general(a, b, (((1,), (1,)), ((), ())), preferred_element_type=F32)


def _dot_tn(a, b):
    return lax.dot_general(a, b, (((0,), (0,)), ((), ())), preferred_element_type=F32)


def _mod_kernel(c_ref, w_ref, b_ref, o_ref):
    c = c_ref[...]
    ca = (c * _sigmoid(c)).astype(BF16)
    o_ref[...] = _dot(ca, w_ref[...].astype(BF16)) + b_ref[...]


def _mod_call(c, w, b):
    bsz, d = c.shape
    n = w.shape[1]
    tn = MOD_COL_TILE
    return pl.pallas_call(
        _mod_kernel,
        grid=(n // tn,),
        in_specs=[
            pl.BlockSpec((bsz, d), lambda j: (0, 0)),
            pl.BlockSpec((d, tn), lambda j: (0, j)),
            pl.BlockSpec((1, tn), lambda j: (0, j)),
        ],
        out_specs=pl.BlockSpec((bsz, tn), lambda j: (0, j)),
        out_shape=jax.ShapeDtypeStruct((bsz, n), F32),
        compiler_params=pltpu.CompilerParams(dimension_semantics=("arbitrary",)),
        name="mod",
    )(c, w, b.reshape(1, n))


def _rope_kernel(pos_ref, inv_ref, cos_ref, sin_ref):
    tr = pos_ref.shape[0]
    ang = pos_ref[...].astype(F32) * inv_ref[...]
    group = lax.broadcasted_iota(jnp.int32, (tr, LANES), 1) // ROPE_FREQS
    sign = jnp.where(group % 2 == 0, -1.0, 1.0)
    n_grp = LANES // ROPE_FREQS
    for tab, out_ref, signed in ((jnp.cos(ang), cos_ref, False), (jnp.sin(ang), sin_ref, True)):
        rolled = [tab] + [pltpu.roll(tab, ROPE_FREQS * k, 1) for k in range(1, n_grp)]
        for a in range(n_grp):
            piece = rolled[(0 - a) % n_grp]
            for kk in range(1, n_grp):
                piece = jnp.where(group == kk, rolled[(kk - a) % n_grp], piece)
            if signed:
                piece = piece * sign
            out_ref[pl.ds(a, tr, stride=n_grp), :] = piece


def _rope_call(pos_rep, inv_row):
    rows = pos_rep.shape[0]
    n_grp = LANES // ROPE_FREQS
    tr = min(rows, 512)
    out_spec = pl.BlockSpec((n_grp * tr, LANES), lambda i: (i, 0))
    return pl.pallas_call(
        _rope_kernel,
        grid=(rows // tr,),
        in_specs=[pl.BlockSpec((tr, LANES), lambda i: (i, 0)),
                  pl.BlockSpec((1, LANES), lambda i: (0, 0))],
        out_specs=[out_spec, out_spec],
        out_shape=[jax.ShapeDtypeStruct((n_grp * rows, LANES), F32)] * 2,
        compiler_params=pltpu.CompilerParams(dimension_semantics=("arbitrary",)),
        name="rope",
    )(pos_rep, inv_row)


def _resident(shape, index=None):
    index = (0,) * len(shape) if index is None else index
    return pl.BlockSpec(shape, lambda b, s: index, pipeline_mode=pl.Buffered(1))


def _ffn_kernel(x_ref, mod_ref, npre_ref, npost_ref, win_ref, wout_ref, o_ref, a_scr, *, sub, d_ff):
    sh = mod_ref[0, 3 * sub:3 * sub + 1, :]
    sc = mod_ref[0, 3 * sub + 1:3 * sub + 2, :]
    gt = mod_ref[0, 3 * sub + 2:3 * sub + 3, :]
    tm = x_ref.shape[1]
    n_chunks = d_ff // FFN_COL_TILE
    gain = npre_ref[...] * (1.0 + sc)
    out_gain = (FFN_RES_WEIGHT * gt) * npost_ref[...]

    def gated(h_rows, j):
        lo = j * FFN_COL_TILE
        g = _dot(h_rows, win_ref[0, :, lo:lo + FFN_COL_TILE])
        u = _dot(h_rows, win_ref[0, :, d_ff + lo:d_ff + lo + FFN_COL_TILE])
        return (_silu(g) * u).astype(BF16)

    slabs = [slice(r, r + FFN_EDGE_ROWS) for r in range(0, tm, FFN_EDGE_ROWS)]
    h_parts = []
    for rows in slabs:
        h_rows = _modulated(x_ref[0, rows, :], gain, sh)
        h_parts.append(h_rows)
        a_scr[rows, 0:FFN_COL_TILE] = gated(h_rows, 0)
    h = jnp.concatenate(h_parts, axis=0)
    for j in range(1, n_chunks):
        a_scr[:, j * FFN_COL_TILE:(j + 1) * FFN_COL_TILE] = gated(h, j)
    for rows in slabs:
        y = _dot(a_scr[rows, :], wout_ref[0])
        o_ref[0, rows, :] = x_ref[0, rows, :] + _rms(y) * out_gain


def _ffn_call(x, mod, npre, npost, w_in, w_out, *, which, tm):
    bsz, seq, d = x.shape
    d_ff = w_out.shape[1]
    sub = 2 * which
    xspec = pl.BlockSpec((1, tm, d), lambda b, s: (b, s, 0))
    return pl.pallas_call(
        functools.partial(_ffn_kernel, sub=sub, d_ff=d_ff),
        grid=(bsz, seq // tm),
        in_specs=[
            xspec,
            pl.BlockSpec((1, N_MOD, d), lambda b, s: (b, 0, 0)),
            _resident((1, d)),
            _resident((1, d)),
            _resident((1, d, 2 * d_ff), (which, 0, 0)),
            _resident((1, d_ff, d), (which, 0, 0)),
        ],
        out_specs=xspec,
        out_shape=jax.ShapeDtypeStruct(x.shape, F32),
        scratch_shapes=[pltpu.VMEM((tm, d_ff), BF16)],
        compiler_params=pltpu.CompilerParams(
            dimension_semantics=("arbitrary", "arbitrary"),
            vmem_limit_bytes=VMEM_LIMIT_BYTES),
        name="ffn%d" % sub,
    )(x, mod, npre, npost, w_in, w_out)


def _mixer_kernel(sinks_ref, x_ref, mod_ref, npre_ref, npost_ref, win_ref, wout_ref,
                  cos_ref, sin_ref, lbl_ref, gn_ref, tril_ref, o_ref,
                  kbuf, vbuf, st_ref, mix_scr, *, tm, layer):
    si = pl.program_id(1)
    n_groups = tm // GROUP_ROWS
    n_spans = tm // HGRN_SPAN

    lane = lax.broadcasted_iota(jnp.int32, (tm, LANES), 1)
    row = lax.broadcasted_iota(jnp.int32, (tm, LANES), 0)
    low_head = lane < ATTN_HEAD_DIM

    @pl.when(si == 0)
    def _():
        carry_lane = lax.broadcasted_iota(jnp.int32, (GROUP_ROWS, LANES), 1)
        k0 = jnp.where(carry_lane == FLAG_LANE, 1.0, 0.0).astype(BF16)
        v0 = jnp.where(carry_lane < ATTN_HEAD_DIM, 0.0, 1.0).astype(BF16)
        for hk in range(ATTN_KV_HEADS):
            kbuf[hk, 0:GROUP_ROWS, :] = k0
            vbuf[hk, 0:GROUP_ROWS, :] = v0
        st_ref[...] = jnp.zeros(st_ref.shape, F32)

    x = x_ref[0]
    sh = mod_ref[0, 3:4, :]
    sc = mod_ref[0, 4:5, :]
    gt = mod_ref[0, 5:6, :]
    h = _modulated(x, npre_ref[...] * (1.0 + sc), sh)

    cos = cos_ref[0]
    sin = sin_ref[0]
    first_half = (lane % ATTN_HEAD_DIM) < (ATTN_HEAD_DIM // 2)
    chunk_id = (row // CHUNK) % BAND_CHUNKS
    key_extra = jnp.where(lane == ATTN_HEAD_DIM + chunk_id, 1.0, 0.0)
    hidden = ATTN_HEAD_DIM + (chunk_id + 1) % BAND_CHUNKS
    query_extra = jnp.where((lane == hidden) | (lane == FLAG_LANE), NEG_INF, 0.0)

    def rope(t):
        rot = jnp.where(first_half, pltpu.roll(t, LANES - 32, 1), pltpu.roll(t, 32, 1))
        return t * cos + rot * sin

    aq = _dot(h, win_ref[:, OFF_AQ:OFF_AQ + ATTN_WIDTH])
    akv = _dot(h, win_ref[:, OFF_AKV:OFF_AKV + 2 * KV_WIDTH])
    k = rope(akv[:, 0:KV_WIDTH])
    v = akv[:, KV_WIDTH:2 * KV_WIDTH]
    for src, buf, extra in ((k, kbuf, key_extra), (v, vbuf, 1.0)):
        buf[0, GROUP_ROWS:GROUP_ROWS + tm, :] = jnp.where(low_head, src, extra).astype(BF16)
        buf[1, GROUP_ROWS:GROUP_ROWS + tm, :] = jnp.where(
            low_head, pltpu.roll(src, ATTN_HEAD_DIM, 1), extra).astype(BF16)

    qh = []
    for jb in range(ATTN_Q_HEADS // 2):
        qb = rope(aq[:, jb * LANES:(jb + 1) * LANES]) * (ATTN_HEAD_DIM ** -0.5 * LOG2E)
        qh.append(jnp.where(low_head, qb, query_extra).astype(BF16))
        qh.append(jnp.where(low_head, pltpu.roll(qb, ATTN_HEAD_DIM, 1), query_extra).astype(BF16))

    low_grp = lax.broadcasted_iota(jnp.int32, (GROUP_ROWS, LANES), 1) < ATTN_HEAD_DIM

    def attn_block(g, hk):
        r0 = g * GROUP_ROWS
        heads = [hk * ATTN_GROUP + i for i in range(ATTN_GROUP)]
        qs = jnp.concatenate([qh[hd][r0:r0 + GROUP_ROWS, :] for hd in heads], axis=0)
        kband = kbuf[hk, r0:r0 + BAND_ROWS, :]
        vband = vbuf[hk, r0:r0 + BAND_ROWS, :]
        s = _dot_nt(qs, kband)
        m = jnp.max(s, axis=-1, keepdims=True)
        p = jnp.exp2(s - m)
        o = _dot(p.astype(BF16), vband)
        for pair in range(ATTN_GROUP // 2):
            res = []
            for odd in range(2):
                a0 = (2 * pair + odd) * GROUP_ROWS
                o_h = o[a0:a0 + GROUP_ROWS, :]
                sink_term = jnp.exp2(sinks_ref[heads[2 * pair + odd]] * LOG2E
                                     - m[a0:a0 + GROUP_ROWS, :])
                swapped = pltpu.roll(o_h, ATTN_HEAD_DIM, 1)
                if odd:
                    res.append(swapped / (o_h + sink_term))
                else:
                    res.append(o_h / (swapped + sink_term))
            col = (hk * (ATTN_GROUP // 2) + pair) * LANES
            mix_scr[r0:r0 + GROUP_ROWS, col:col + LANES] = (
                jnp.where(low_grp, res[0], res[1]).astype(BF16))

    lbl = lbl_ref[...]
    e_lb = jnp.exp(lbl - jnp.max(lbl, axis=0, keepdims=True))
    lb_all = jnp.sum(e_lb[0:layer + 1, :], axis=0, keepdims=True) / jnp.sum(e_lb, axis=0, keepdims=True)

    hq = _dot(h, win_ref[:, OFF_HQ:OFF_HQ + HGRN_WIDTH])
    hf = _dot(h, win_ref[:, OFF_HF:OFF_HF + HGRN_WIDTH])
    hi = _dot(h, win_ref[:, OFF_HI:OFF_HI + HGRN_WIDTH])
    hg = _dot(h, win_ref[:, OFF_HG:OFF_HG + HGRN_WIDTH])

    f = lb_all + (1.0 - lb_all) * _sigmoid(hf)
    one_minus_f = 1.0 - f
    g = jnp.log2(f)
    g1 = g.astype(BF16)
    g2 = (g - g1.astype(F32)).astype(BF16)
    tril = tril_ref[...]
    b_spans = []
    for sp in range(n_spans):
        rows = slice(sp * HGRN_SPAN, (sp + 1) * HGRN_SPAN)
        b_spans.append(_dot(tril, g1[rows, :]) + _dot(tril, g2[rows, :]))

    half_span = HGRN_SPAN // 2
    level_halves = []
    width = half_span
    while width >= DIRECT_SPAN:
        level_halves.append(width)
        width //= 2
    ti = lax.broadcasted_iota(jnp.int32, (half_span, half_span), 0)
    sj = lax.broadcasted_iota(jnp.int32, (half_span, half_span), 1)
    quad_mask = {}
    for half in level_halves[1:]:
        blk = 2 * half
        quad_mask[half] = ((ti // blk) == (sj // blk)) & ((ti % blk) >= half) & ((sj % blk) < half)
    diag = [sj == ti - delta for delta in range(DIRECT_SPAN)]
    sub_row = lax.broadcasted_iota(jnp.int32, (HGRN_SPAN, LANES), 0) % SUBLANES
    later4 = (sub_row % (2 * DIRECT_SPAN)) >= DIRECT_SPAN
    in_vreg = lax.broadcasted_iota(jnp.int32, (1, SUBLANES, LANES), 1)
    direct_ok = [(in_vreg % DIRECT_SPAN) >= delta for delta in range(DIRECT_SPAN)]

    gn = gn_ref[...]

    def hgrn_unit(hd, sp):
        cols = slice(hd * HGRN_DIM, (hd + 1) * HGRN_DIM)
        base = sp * HGRN_SPAN
        rows = slice(base, base + HGRN_SPAN)
        rq = _silu(hq[rows, cols]) * (HGRN_DIM ** -0.5)
        rk = one_minus_f[rows, cols]
        rv = hi[rows, cols].astype(BF16)
        gate = _silu(hg[rows, cols])
        b = b_spans[sp][:, cols]

        lvl_w = {}
        for half in level_halves:
            blk = 2 * half
            expo, operand = [], []
            for mblk in range(HGRN_SPAN // blk):
                lo = mblk * blk
                ref_row = b[lo + half - 1:lo + half, :]
                if half >= SUBLANES:
                    ref_half = jnp.broadcast_to(ref_row, (half, HGRN_DIM))
                    expo += [ref_half - b[lo:lo + half, :], b[lo + half:lo + blk, :] - ref_half]
                    operand += [rk[lo:lo + half, :], rq[lo + half:lo + blk, :]]
                else:
                    expo.append(jnp.broadcast_to(ref_row, (blk, HGRN_DIM)))
            if half >= SUBLANES:
                w = jnp.concatenate(operand, axis=0) * jnp.exp2(jnp.concatenate(expo, axis=0))
            else:
                dist = jnp.abs(b - jnp.concatenate(expo, axis=0))
                w = jnp.where(later4, rq, rk) * jnp.exp2(-dist)
            lvl_w[half] = w.astype(BF16)

        def grouped(t):
            return t.reshape(HGRN_SPAN // SUBLANES, SUBLANES, HGRN_DIM)

        b3, rq3, rk3 = grouped(b), grouped(rq), grouped(rk)
        direct = []
        for delta in range(DIRECT_SPAN):
            if delta == 0:
                prod = rq * rk
            else:
                gap = jnp.where(direct_ok[delta], b3 - pltpu.roll(b3, delta, 1), NEG_INF)
                prod = (rq3 * pltpu.roll(rk3, delta, 1) * jnp.exp2(gap)).reshape(HGRN_SPAN, HGRN_DIM)
            direct.append(jnp.sum(prod, axis=-1, keepdims=True))

        top = _dot_nt(lvl_w[half_span][half_span:, :], lvl_w[half_span][0:half_span, :])
        quads = []
        for qd in range(2):
            o0 = qd * half_span
            sc_q = jnp.zeros((half_span, half_span), F32)
            for delta in range(DIRECT_SPAN):
                sc_q = jnp.where(diag[delta], direct[delta][o0:o0 + half_span, :], sc_q)
            for half in level_halves[1:]:
                w_q = lvl_w[half][o0:o0 + half_span, :]
                sc_q = jnp.where(quad_mask[half], _dot_nt(w_q, w_q), sc_q)
            quads.append(sc_q.astype(BF16))
        s_hi = jnp.concatenate([top.astype(BF16), quads[1]], axis=1)

        st = st_ref[hd]
        b_last = b[HGRN_SPAN - 1:HGRN_SPAN, :]
        q_in = (rq * jnp.exp2(b)).astype(BF16)
        k_out = (rk * jnp.exp2(b_last - b)).astype(BF16)
        o = _dot_nt(q_in, st.astype(BF16)) + jnp.concatenate(
            [_dot(quads[0], rv[0:half_span, :]), _dot(s_hi, rv)], axis=0)
        st_ref[hd] = st * jnp.exp2(b_last) + _dot_tn(rv, k_out)
        o = _rms(o) * gn * gate
        mix_scr[rows, ATTN_WIDTH + hd * HGRN_DIM:ATTN_WIDTH + (hd + 1) * HGRN_DIM] = o.astype(BF16)

    attn_blocks = [(g, hk) for g in range(n_groups) for hk in range(ATTN_KV_HEADS)]
    hgrn_units = [(hd, sp) for sp in range(n_spans) for hd in range(HGRN_HEADS)]
    for i in range(max(len(attn_blocks), len(hgrn_units))):
        if i < len(attn_blocks):
            attn_block(*attn_blocks[i])
        if i < len(hgrn_units):
            hgrn_unit(*hgrn_units[i])

    for hk in range(ATTN_KV_HEADS):
        kbuf[hk, 0:GROUP_ROWS, :] = kbuf[hk, tm:tm + GROUP_ROWS, :]
        vbuf[hk, 0:GROUP_ROWS, :] = vbuf[hk, tm:tm + GROUP_ROWS, :]

    y = _dot(mix_scr[...], wout_ref[...])
    o_ref[0] = x_ref[0] + _rms(y) * (gt * npost_ref[...])


def _mixer_call(x, mod, npre, npost, w_in, w_out, cos_t, sin_t, sinks, lb_logits, gnorm, *, tm, layer):
    bsz, seq, d = x.shape
    d_proj = w_in.shape[1]
    d_mix = w_out.shape[0]
    assert tm % (BAND_CHUNKS * CHUNK) == 0 and tm % HGRN_SPAN == 0
    r = jnp.arange(HGRN_SPAN)
    tril = (r[None, :] <= r[:, None]).astype(BF16)
    xspec = pl.BlockSpec((1, tm, d), lambda b, s: (b, s, 0))
    tspec = pl.BlockSpec((1, tm, LANES), lambda b, s: (b, s, 0))
    return pl.pallas_call(
        functools.partial(_mixer_kernel, tm=tm, layer=layer),
        grid=(bsz, seq // tm),
        in_specs=[
            pl.BlockSpec(memory_space=pltpu.SMEM),
            xspec,
            pl.BlockSpec((1, N_MOD, d), lambda b, s: (b, 0, 0)),
            _resident((1, d)),
            _resident((1, d)),
            _resident((d, d_proj)),
            _resident((d_mix, d)),
            tspec,
            tspec,
            _resident(lb_logits.shape),
            _resident((1, HGRN_DIM)),
            _resident((HGRN_SPAN, HGRN_SPAN)),
        ],
        out_specs=xspec,
        out_shape=jax.ShapeDtypeStruct(x.shape, F32),
        scratch_shapes=[
            pltpu.VMEM((ATTN_KV_HEADS, GROUP_ROWS + tm, LANES), BF16),
            pltpu.VMEM((ATTN_KV_HEADS, GROUP_ROWS + tm, LANES), BF16),
            pltpu.VMEM((HGRN_HEADS, HGRN_DIM, HGRN_DIM), F32),
            pltpu.VMEM((tm, d_mix), BF16),
        ],
        compiler_params=pltpu.CompilerParams(
            dimension_semantics=("arbitrary", "arbitrary"),
            vmem_limit_bytes=VMEM_LIMIT_BYTES),
        name="mixer",
    )(sinks, x, mod, npre, npost, w_in, w_out, cos_t, sin_t, lb_logits, gnorm, tril)


def kernel(x, c, positions, w_cond, b_cond, norm_pre, norm_post, ffn_w_in, ffn_w_out,
           w_mix_in, w_mix_out, attn_sinks, hgrn_lb_logits, hgrn_gnorm):
    bsz, seq, d = x.shape
    depth = w_cond.shape[0]
    tm_ffn = min(seq, 1024)
    tm_mix = min(seq, 512)

    inv_freq = 1.0 / (ROPE_THETA ** (jnp.arange(0, ATTN_HEAD_DIM, 2, dtype=F32) / ATTN_HEAD_DIM))
    inv_row = jnp.tile(inv_freq, LANES // ROPE_FREQS)[None, :]
    pos_rep = jnp.broadcast_to(positions.reshape(-1, 1), (bsz * seq, ROPE_FREQS)).reshape(-1, LANES)
    cos_t, sin_t = _rope_call(pos_rep, inv_row)
    cos_t = cos_t.reshape(bsz, seq, LANES)
    sin_t = sin_t.reshape(bsz, seq, LANES)

    w_ffn_in = ffn_w_in.astype(BF16)
    w_ffn_out = ffn_w_out.astype(BF16)
    for layer in range(depth):
        mod = _mod_call(c, w_cond[layer], b_cond[layer]).reshape(bsz, N_MOD, d)
        npre = norm_pre[layer][:, None, :]
        npost = norm_post[layer][:, None, :]
        x = _ffn_call(x, mod, npre[0], npost[0], w_ffn_in[layer], w_ffn_out[layer],
                      which=0, tm=tm_ffn)
        x = _mixer_call(x, mod, npre[1], npost[1], w_mix_in[layer].astype(BF16),
                        w_mix_out[layer].astype(BF16), cos_t, sin_t, attn_sinks[layer],
                        hgrn_lb_logits, hgrn_gnorm[layer][None, :], tm=tm_mix, layer=layer)
        x = _ffn_call(x, mod, npre[2], npost[2], w_ffn_in[layer], w_ffn_out[layer],
                      which=1, tm=tm_ffn)
    return x
```

```python
import functools

import jax
import jax.numpy as jnp
from jax import lax
from jax.experimental import pallas as pl
from jax.experimental.pallas import tpu as pltpu

F32 = jnp.float32
BF16 = jnp.bfloat16

EPS = 1e-6
NEG_INF = -1e30
ROPE_THETA = 10000.0
LOG2E = 1.4426950408889634

CHUNK = 64
ATTN_HEAD_DIM = 64
ATTN_Q_HEADS = 8
ATTN_KV_HEADS = 2
ATTN_GROUP = ATTN_Q_HEADS // ATTN_KV_HEADS
ATTN_WIDTH = ATTN_Q_HEADS * ATTN_HEAD_DIM
KV_WIDTH = ATTN_KV_HEADS * ATTN_HEAD_DIM
HGRN_HEADS = 4
HGRN_DIM = 128
HGRN_WIDTH = HGRN_HEADS * HGRN_DIM
N_MOD = 9
FFN_RES_WEIGHT = 0.5

OFF_AQ = 0
OFF_AKV = OFF_AQ + ATTN_WIDTH
OFF_HQ = OFF_AKV + 2 * KV_WIDTH
OFF_HF = OFF_HQ + HGRN_WIDTH
OFF_HI = OFF_HF + HGRN_WIDTH
OFF_HG = OFF_HI + HGRN_WIDTH

LANES = 128
SUBLANES = 8
GROUP_ROWS = 2 * CHUNK
BAND_CHUNKS = 4
BAND_ROWS = BAND_CHUNKS * CHUNK
FFN_COL_TILE = 256
FFN_EDGE_ROWS = 512
MOD_COL_TILE = 2304
ROPE_FREQS = ATTN_HEAD_DIM // 2
HGRN_SPAN = 256
VMEM_LIMIT_BYTES = 56 * 1024 * 1024

FLAG_LANE = ATTN_HEAD_DIM + BAND_CHUNKS

DIRECT_SPAN = 4


def _sigmoid(v):
    return 1.0 / (1.0 + jnp.exp(-v))


def _silu(v):
    hv = 0.5 * v
    return hv + hv * jnp.tanh(hv)


def _rms(v):
    return v * lax.rsqrt(jnp.mean(v * v, axis=-1, keepdims=True) + EPS)


def _modulated(x, gain, shift):
    return _rms(x).astype(BF16) * gain.astype(BF16) + shift.astype(BF16)


def _dot(a, b):
    return jnp.dot(a, b, preferred_element_type=F32)


def _dot_nt(a, b):
    return lax.dot_general(a, b, (((1,), (1,)), ((), ())), preferred_element_type=F32)


def _dot_tn(a, b):
    return lax.dot_general(a, b, (((0,), (0,)), ((), ())), preferred_element_type=F32)


def _mod_kernel(c_ref, w_ref, b_ref, o_ref):
    c = c_ref[...]
    ca = (c * _sigmoid(c)).astype(BF16)
    o_ref[...] = _dot(ca, w_ref[...].astype(BF16)) + b_ref[...]


def _mod_call(c, w, b):
    bsz, d = c.shape
    n = w.shape[1]
    tn = MOD_COL_TILE
    return pl.pallas_call(
        _mod_kernel,
        grid=(n // tn,),
        in_specs=[
            pl.BlockSpec((bsz, d), lambda j: (0, 0)),
            pl.BlockSpec((d, tn), lambda j: (0, j)),
            pl.BlockSpec((1, tn), lambda j: (0, j)),
        ],
        out_specs=pl.BlockSpec((bsz, tn), lambda j: (0, j)),
        out_shape=jax.ShapeDtypeStruct((bsz, n), F32),
        compiler_params=pltpu.CompilerParams(dimension_semantics=("arbitrary",)),
        name="mod",
    )(c, w, b.reshape(1, n))


def _rope_kernel(pos_ref, inv_ref, cos_ref, sin_ref):
    tr = pos_ref.shape[0]
    ang = pos_ref[...].astype(F32) * inv_ref[...]
    group = lax.broadcasted_iota(jnp.int32, (tr, LANES), 1) // ROPE_FREQS
    sign = jnp.where(group % 2 == 0, -1.0, 1.0)
    n_grp = LANES // ROPE_FREQS
    for tab, out_ref, signed in ((jnp.cos(ang), cos_ref, False), (jnp.sin(ang), sin_ref, True)):
        rolled = [tab] + [pltpu.roll(tab, ROPE_FREQS * k, 1) for k in range(1, n_grp)]
        for a in range(n_grp):
            piece = rolled[(0 - a) % n_grp]
            for kk in range(1, n_grp):
                piece = jnp.where(group == kk, rolled[(kk - a) % n_grp], piece)
            if signed:
                piece = piece * sign
            out_ref[pl.ds(a, tr, stride=n_grp), :] = piece


def _rope_call(pos_rep, inv_row):
    rows = pos_rep.shape[0]
    n_grp = LANES // ROPE_FREQS
    tr = min(rows, 512)
    out_spec = pl.BlockSpec((n_grp * tr, LANES), lambda i: (i, 0))
    return pl.pallas_call(
        _rope_kernel,
        grid=(rows // tr,),
        in_specs=[pl.BlockSpec((tr, LANES), lambda i: (i, 0)),
                  pl.BlockSpec((1, LANES), lambda i: (0, 0))],
        out_specs=[out_spec, out_spec],
        out_shape=[jax.ShapeDtypeStruct((n_grp * rows, LANES), F32)] * 2,
        compiler_params=pltpu.CompilerParams(dimension_semantics=("arbitrary",)),
        name="rope",
    )(pos_rep, inv_row)


def _resident(shape, index=None):
    index = (0,) * len(shape) if index is None else index
    return pl.BlockSpec(shape, lambda b, s: index, pipeline_mode=pl.Buffered(1))


def _ffn_kernel(x_ref, mod_ref, npre_ref, npost_ref, win_ref, wout_ref, o_ref, a_scr, *, sub, d_ff):
    sh = mod_ref[0, 3 * sub:3 * sub + 1, :]
    sc = mod_ref[0, 3 * sub + 1:3 * sub + 2, :]
    gt = mod_ref[0, 3 * sub + 2:3 * sub + 3, :]
    tm = x_ref.shape[1]
    n_chunks = d_ff // FFN_COL_TILE
    gain = npre_ref[...] * (1.0 + sc)
    out_gain = (FFN_RES_WEIGHT * gt) * npost_ref[...]

    def gated(h_rows, lo, hi):
        g = _dot(h_rows, win_ref[0, :, lo:hi])
        u = _dot(h_rows, win_ref[0, :, d_ff + lo:d_ff + hi])
        return (_silu(g) * u).astype(BF16)

    wide = 2 * FFN_COL_TILE
    bounds = [(lo, min(lo + wide, d_ff)) for lo in range(0, d_ff, wide)]
    slabs = [slice(r, r + FFN_EDGE_ROWS) for r in range(0, tm, FFN_EDGE_ROWS)]
    h_parts = []
    for rows in slabs:
        h_rows = _modulated(x_ref[0, rows, :], gain, sh)
        h_parts.append(h_rows)
        a_scr[rows, bounds[0][0]:bounds[0][1]] = gated(h_rows, *bounds[0])
    h = jnp.concatenate(h_parts, axis=0)
    for lo, hi in bounds[1:]:
        a_scr[:, lo:hi] = gated(h, lo, hi)
    for rows in slabs:
        y = _dot(a_scr[rows, :], wout_ref[0])
        o_ref[0, rows, :] = x_ref[0, rows, :] + _rms(y) * out_gain


def _ffn_call(x, mod, npre, npost, w_in, w_out, *, which, tm):
    bsz, seq, d = x.shape
    d_ff = w_out.shape[1]
    sub = 2 * which
    xspec = pl.BlockSpec((1, tm, d), lambda b, s: (b, s, 0))
    return pl.pallas_call(
        functools.partial(_ffn_kernel, sub=sub, d_ff=d_ff),
        grid=(bsz, seq // tm),
        in_specs=[
            xspec,
            pl.BlockSpec((1, N_MOD, d), lambda b, s: (b, 0, 0)),
            _resident((1, d)),
            _resident((1, d)),
            _resident((1, d, 2 * d_ff), (which, 0, 0)),
            _resident((1, d_ff, d), (which, 0, 0)),
        ],
        out_specs=xspec,
        out_shape=jax.ShapeDtypeStruct(x.shape, F32),
        scratch_shapes=[pltpu.VMEM((tm, d_ff), BF16)],
        compiler_params=pltpu.CompilerParams(
            dimension_semantics=("arbitrary", "arbitrary"),
            vmem_limit_bytes=VMEM_LIMIT_BYTES),
        name="ffn%d" % sub,
    )(x, mod, npre, npost, w_in, w_out)


def _mixer_kernel(sinks_ref, x_ref, mod_ref, npre_ref, npost_ref, win_ref, wout_ref,
                  cos_ref, sin_ref, lbl_ref, gn_ref, tril_ref, o_ref,
                  kbuf, vbuf, st_ref, mix_scr, *, tm, layer):
    si = pl.program_id(1)
    n_groups = tm // GROUP_ROWS
    n_spans = tm // HGRN_SPAN

    lane = lax.broadcasted_iota(jnp.int32, (tm, LANES), 1)
    row = lax.broadcasted_iota(jnp.int32, (tm, LANES), 0)
    low_head = lane < ATTN_HEAD_DIM

    @pl.when(si == 0)
    def _():
        carry_lane = lax.broadcasted_iota(jnp.int32, (GROUP_ROWS, LANES), 1)
        k0 = jnp.where(carry_lane == FLAG_LANE, 1.0, 0.0).astype(BF16)
        v0 = jnp.where(carry_lane < ATTN_HEAD_DIM, 0.0, 1.0).astype(BF16)
        for hk in range(ATTN_KV_HEADS):
            kbuf[hk, 0:GROUP_ROWS, :] = k0
            vbuf[hk, 0:GROUP_ROWS, :] = v0
        st_ref[...] = jnp.zeros(st_ref.shape, F32)

    x = x_ref[0]
    sh = mod_ref[0, 3:4, :]
    sc = mod_ref[0, 4:5, :]
    gt = mod_ref[0, 5:6, :]
    h = _modulated(x, npre_ref[...] * (1.0 + sc), sh)

    cos = cos_ref[0]
    sin = sin_ref[0]
    first_half = (lane % ATTN_HEAD_DIM) < (ATTN_HEAD_DIM // 2)
    chunk_id = (row // CHUNK) % BAND_CHUNKS
    key_extra = jnp.where(lane == ATTN_HEAD_DIM + chunk_id, 1.0, 0.0)
    hidden = ATTN_HEAD_DIM + (chunk_id + 1) % BAND_CHUNKS
    query_extra = jnp.where((lane == hidden) | (lane == FLAG_LANE), NEG_INF, 0.0)

    def rope(t):
        rot = jnp.where(first_half, pltpu.roll(t, LANES - 32, 1), pltpu.roll(t, 32, 1))
        return t * cos + rot * sin

    aq = _dot(h, win_ref[:, OFF_AQ:OFF_AQ + ATTN_WIDTH])
    akv = _dot(h, win_ref[:, OFF_AKV:OFF_AKV + 2 * KV_WIDTH])
    k = rope(akv[:, 0:KV_WIDTH])
    v = akv[:, KV_WIDTH:2 * KV_WIDTH]
    for src, buf, extra in ((k, kbuf, key_extra), (v, vbuf, 1.0)):
        buf[0, GROUP_ROWS:GROUP_ROWS + tm, :] = jnp.where(low_head, src, extra).astype(BF16)
        buf[1, GROUP_ROWS:GROUP_ROWS + tm, :] = jnp.where(
            low_head, pltpu.roll(src, ATTN_HEAD_DIM, 1), extra).astype(BF16)

    qh = []
    for jb in range(ATTN_Q_HEADS // 2):
        qb = rope(aq[:, jb * LANES:(jb + 1) * LANES]) * (ATTN_HEAD_DIM ** -0.5 * LOG2E)
        qh.append(jnp.where(low_head, qb, query_extra).astype(BF16))
        qh.append(jnp.where(low_head, pltpu.roll(qb, ATTN_HEAD_DIM, 1), query_extra).astype(BF16))

    low_grp = lax.broadcasted_iota(jnp.int32, (GROUP_ROWS, LANES), 1) < ATTN_HEAD_DIM

    def attn_block(g, hk):
        r0 = g * GROUP_ROWS
        heads = [hk * ATTN_GROUP + i for i in range(ATTN_GROUP)]
        qs = jnp.concatenate([qh[hd][r0:r0 + GROUP_ROWS, :] for hd in heads], axis=0)
        kband = kbuf[hk, r0:r0 + BAND_ROWS, :]
        vband = vbuf[hk, r0:r0 + BAND_ROWS, :]
        s = _dot_nt(qs, kband)
        m = jnp.max(s, axis=-1, keepdims=True)
        p = jnp.exp2(s - m)
        o = _dot(p.astype(BF16), vband)
        for pair in range(ATTN_GROUP // 2):
            res = []
            for odd in range(2):
                a0 = (2 * pair + odd) * GROUP_ROWS
                o_h = o[a0:a0 + GROUP_ROWS, :]
                sink_term = jnp.exp2(sinks_ref[heads[2 * pair + odd]] * LOG2E
                                     - m[a0:a0 + GROUP_ROWS, :])
                swapped = pltpu.roll(o_h, ATTN_HEAD_DIM, 1)
                if odd:
                    res.append(swapped / (o_h + sink_term))
                else:
                    res.append(o_h / (swapped + sink_term))
            col = (hk * (ATTN_GROUP // 2) + pair) * LANES
            mix_scr[r0:r0 + GROUP_ROWS, col:col + LANES] = (
                jnp.where(low_grp, res[0], res[1]).astype(BF16))

    lbl = lbl_ref[...]
    e_lb = jnp.exp(lbl - jnp.max(lbl, axis=0, keepdims=True))
    lb_all = jnp.sum(e_lb[0:layer + 1, :], axis=0, keepdims=True) / jnp.sum(e_lb, axis=0, keepdims=True)

    hq = _dot(h, win_ref[:, OFF_HQ:OFF_HQ + HGRN_WIDTH])
    hf = _dot(h, win_ref[:, OFF_HF:OFF_HF + HGRN_WIDTH])
    hi = _dot(h, win_ref[:, OFF_HI:OFF_HI + HGRN_WIDTH])
    hg = _dot(h, win_ref[:, OFF_HG:OFF_HG + HGRN_WIDTH])

    f = lb_all + (1.0 - lb_all) * _sigmoid(hf)
    one_minus_f = 1.0 - f
    g = jnp.log2(f)
    g1 = g.astype(BF16)
    g2 = (g - g1.astype(F32)).astype(BF16)
    tril = tril_ref[...]
    b_spans = []
    for sp in range(n_spans):
        rows = slice(sp * HGRN_SPAN, (sp + 1) * HGRN_SPAN)
        b_spans.append(_dot(tril, g1[rows, :]) + _dot(tril, g2[rows, :]))

    half_span = HGRN_SPAN // 2
    level_halves = []
    width = half_span
    while width >= DIRECT_SPAN:
        level_halves.append(width)
        width //= 2
    ti = lax.broadcasted_iota(jnp.int32, (half_span, half_span), 0)
    sj = lax.broadcasted_iota(jnp.int32, (half_span, half_span), 1)
    quad_mask = {}
    for half in level_halves[1:]:
        blk = 2 * half
        quad_mask[half] = ((ti // blk) == (sj // blk)) & ((ti % blk) >= half) & ((sj % blk) < half)
    diag = [sj == ti - delta for delta in range(DIRECT_SPAN)]
    sub_row = lax.broadcasted_iota(jnp.int32, (HGRN_SPAN, LANES), 0) % SUBLANES
    later4 = (sub_row % (2 * DIRECT_SPAN)) >= DIRECT_SPAN
    in_vreg = lax.broadcasted_iota(jnp.int32, (1, SUBLANES, LANES), 1)
    direct_ok = [(in_vreg % DIRECT_SPAN) >= delta for delta in range(DIRECT_SPAN)]

    gn = gn_ref[...]

    def hgrn_unit(hd, sp):
        cols = slice(hd * HGRN_DIM, (hd + 1) * HGRN_DIM)
        base = sp * HGRN_SPAN
        rows = slice(base, base + HGRN_SPAN)
        rq = _silu(hq[rows, cols]) * (HGRN_DIM ** -0.5)
        rk = one_minus_f[rows, cols]
        rv = hi[rows, cols].astype(BF16)
        gate = _silu(hg[rows, cols])
        b = b_spans[sp][:, cols]

        lvl_w = {}
        for half in level_halves:
            blk = 2 * half
            expo, operand = [], []
            for mblk in range(HGRN_SPAN // blk):
                lo = mblk * blk
                ref_row = b[lo + half - 1:lo + half, :]
                if half >= SUBLANES:
                    ref_half = jnp.broadcast_to(ref_row, (half, HGRN_DIM))
                    expo += [ref_half - b[lo:lo + half, :], b[lo + half:lo + blk, :] - ref_half]
                    operand += [rk[lo:lo + half, :], rq[lo + half:lo + blk, :]]
                else:
                    expo.append(jnp.broadcast_to(ref_row, (blk, HGRN_DIM)))
            if half >= SUBLANES:
                w = jnp.concatenate(operand, axis=0) * jnp.exp2(jnp.concatenate(expo, axis=0))
            else:
                dist = jnp.abs(b - jnp.concatenate(expo, axis=0))
                w = jnp.where(later4, rq, rk) * jnp.exp2(-dist)
            lvl_w[half] = w.astype(BF16)

        def grouped(t):
            return t.reshape(HGRN_SPAN // SUBLANES, SUBLANES, HGRN_DIM)

        b3, rq3, rk3 = grouped(b), grouped(rq), grouped(rk)
        direct = []
        for delta in range(DIRECT_SPAN):
            if delta == 0:
                prod = rq * rk
            else:
                gap = jnp.where(direct_ok[delta], b3 - pltpu.roll(b3, delta, 1), NEG_INF)
                prod = (rq3 * pltpu.roll(rk3, delta, 1) * jnp.exp2(gap)).reshape(HGRN_SPAN, HGRN_DIM)
            direct.append(jnp.sum(prod, axis=-1, keepdims=True))

        top = _dot_nt(lvl_w[half_span][half_span:, :], lvl_w[half_span][0:half_span, :])
        quads = []
        for qd in range(2):
            o0 = qd * half_span
            sc_q = jnp.zeros((half_span, half_span), F32)
            for delta in range(DIRECT_SPAN):
                sc_q = jnp.where(diag[delta], direct[delta][o0:o0 + half_span, :], sc_q)
            for half in level_halves[1:]:
                w_q = lvl_w[half][o0:o0 + half_span, :]
                sc_q = jnp.where(quad_mask[half], _dot_nt(w_q, w_q), sc_q)
            quads.append(sc_q.astype(BF16))
        s_hi = jnp.concatenate([top.astype(BF16), quads[1]], axis=1)

        st = st_ref[hd]
        b_last = b[HGRN_SPAN - 1:HGRN_SPAN, :]
        q_in = (rq * jnp.exp2(b)).astype(BF16)
        k_out = (rk * jnp.exp2(b_last - b)).astype(BF16)
        o = _dot_nt(q_in, st.astype(BF16)) + jnp.concatenate(
            [_dot(quads[0], rv[0:half_span, :]), _dot(s_hi, rv)], axis=0)
        st_ref[hd] = st * jnp.exp2(b_last) + _dot_tn(rv, k_out)
        o = _rms(o) * gn * gate
        mix_scr[rows, ATTN_WIDTH + hd * HGRN_DIM:ATTN_WIDTH + (hd + 1) * HGRN_DIM] = o.astype(BF16)

    attn_blocks = [(g, hk) for g in range(n_groups) for hk in range(ATTN_KV_HEADS)]
    hgrn_units = [(hd, sp) for sp in range(n_spans) for hd in range(HGRN_HEADS)]
    for i in range(max(len(attn_blocks), len(hgrn_units))):
        if i < len(attn_blocks):
            attn_block(*attn_blocks[i])
        if i < len(hgrn_units):
            hgrn_unit(*hgrn_units[i])

    for hk in range(ATTN_KV_HEADS):
        kbuf[hk, 0:GROUP_ROWS, :] = kbuf[hk, tm:tm + GROUP_ROWS, :]
        vbuf[hk, 0:GROUP_ROWS, :] = vbuf[hk, tm:tm + GROUP_ROWS, :]

    y = _dot(mix_scr[...], wout_ref[...])
    o_ref[0] = x_ref[0] + _rms(y) * (gt * npost_ref[...])


def _mixer_call(x, mod, npre, npost, w_in, w_out, cos_t, sin_t, sinks, lb_logits, gnorm, *, tm, layer):
    bsz, seq, d = x.shape
    d_proj = w_in.shape[1]
    d_mix = w_out.shape[0]
    assert tm % (BAND_CHUNKS * CHUNK) == 0 and tm % HGRN_SPAN == 0
    r = jnp.arange(HGRN_SPAN)
    tril = (r[None, :] <= r[:, None]).astype(BF16)
    xspec = pl.BlockSpec((1, tm, d), lambda b, s: (b, s, 0))
    tspec = pl.BlockSpec((1, tm, LANES), lambda b, s: (b, s, 0))
    return pl.pallas_call(
        functools.partial(_mixer_kernel, tm=tm, layer=layer),
        grid=(bsz, seq // tm),
        in_specs=[
            pl.BlockSpec(memory_space=pltpu.SMEM),
            xspec,
            pl.BlockSpec((1, N_MOD, d), lambda b, s: (b, 0, 0)),
            _resident((1, d)),
            _resident((1, d)),
            _resident((d, d_proj)),
            _resident((d_mix, d)),
            tspec,
            tspec,
            _resident(lb_logits.shape),
            _resident((1, HGRN_DIM)),
            _resident((HGRN_SPAN, HGRN_SPAN)),
        ],
        out_specs=xspec,
        out_shape=jax.ShapeDtypeStruct(x.shape, F32),
        scratch_shapes=[
            pltpu.VMEM((ATTN_KV_HEADS, GROUP_ROWS + tm, LANES), BF16),
            pltpu.VMEM((ATTN_KV_HEADS, GROUP_ROWS + tm, LANES), BF16),
            pltpu.VMEM((HGRN_HEADS, HGRN_DIM, HGRN_DIM), F32),
            pltpu.VMEM((tm, d_mix), BF16),
        ],
        compiler_params=pltpu.CompilerParams(
            dimension_semantics=("arbitrary", "arbitrary"),
            vmem_limit_bytes=VMEM_LIMIT_BYTES),
        name="mixer",
    )(sinks, x, mod, npre, npost, w_in, w_out, cos_t, sin_t, lb_logits, gnorm, tril)


def kernel(x, c, positions, w_cond, b_cond, norm_pre, norm_post, ffn_w_in, ffn_w_out,
           w_mix_in, w_mix_out, attn_sinks, hgrn_lb_logits, hgrn_gnorm):
    bsz, seq, d = x.shape
    depth = w_cond.shape[0]
    tm_ffn = min(seq, 1024)
    tm_mix = min(seq, 512)

    inv_freq = 1.0 / (ROPE_THETA ** (jnp.arange(0, ATTN_HEAD_DIM, 2, dtype=F32) / ATTN_HEAD_DIM))
    inv_row = jnp.tile(inv_freq, LANES // ROPE_FREQS)[None, :]
    pos_rep = jnp.broadcast_to(positions.reshape(-1, 1), (bsz * seq, ROPE_FREQS)).reshape(-1, LANES)
    cos_t, sin_t = _rope_call(pos_rep, inv_row)
    cos_t = cos_t.reshape(bsz, seq, LANES)
    sin_t = sin_t.reshape(bsz, seq, LANES)

    w_ffn_in = ffn_w_in.astype(BF16)
    w_ffn_out = ffn_w_out.astype(BF16)
    for layer in range(depth):
        mod = _mod_call(c, w_cond[layer], b_cond[layer]).reshape(bsz, N_MOD, d)
        npre = norm_pre[layer][:, None, :]
        npost = norm_post[layer][:, None, :]
        x = _ffn_call(x, mod, npre[0], npost[0], w_ffn_in[layer], w_ffn_out[layer],
                      which=0, tm=tm_ffn)
        x = _mixer_call(x, mod, npre[1], npost[1], w_mix_in[layer].astype(BF16),
                        w_mix_out[layer].astype(BF16), cos_t, sin_t, attn_sinks[layer],
                        hgrn_lb_logits, hgrn_gnorm[layer][None, :], tm=tm_mix, layer=layer)
        x = _ffn_call(x, mod, npre[2], npost[2], w_ffn_in[layer], w_ffn_out[layer],
                      which=1, tm=tm_ffn)
    return x
```
